```python
import math
import jax, jax.numpy as jnp
from jax import lax
import numpy as np

D_MODEL = 2048
BATCH = 2
SEQ = 4096
DEPTH = 2
DEC_BATCH = 128
DEC_SEQ = 1
PAST_LEN = 2048
PAGE_SIZE = 128

MIX_WIDTH = D_MODEL
A_WIDTH = MIX_WIDTH // 2
B_WIDTH = MIX_WIDTH - A_WIDTH
DH_V_A = 128
DH_QK_A = DH_V_A // 2
H_A = A_WIDTH // DH_V_A
DH_B = 256
H_B = B_WIDTH // DH_B
PROJ_WIDTH = 3 * A_WIDTH + 4 * B_WIDTH
D_FF = ((8 * D_MODEL // 3 + 255) // 256) * 256
RET_CHUNK = 128
Q_BLOCK = 128
ROPE_BASE = 10000.0
LN_EPS = 1e-5
ALPHA = (2 * DEPTH) ** 0.25
BETA = (8 * DEPTH) ** -0.25

kernel_name = 'hybrid_diffattn_retention_macaron_deepnorm_step'


def layer_norm(x, g, b):
    xf = x.astype(jnp.float32)
    mu = jnp.mean(xf, axis=-1, keepdims=True)
    var = jnp.mean(jnp.square(xf - mu), axis=-1, keepdims=True)
    return ((xf - mu) * lax.rsqrt(var + LN_EPS) * g.astype(jnp.float32) + b.astype(jnp.float32)).astype(x.dtype)


def rms_norm(x):
    xf = x.astype(jnp.float32)
    return (xf * lax.rsqrt(jnp.mean(jnp.square(xf), axis=-1, keepdims=True) + LN_EPS)).astype(x.dtype)


def macaron_half(x, w1, w3, w2, g, b):
    h = (jax.nn.silu(x @ w1) * (x @ w3)) @ w2
    return layer_norm(ALPHA * x + 0.5 * h, g, b)


def project(x, w_in):
    bsz, s_len = x.shape[:2]
    h = x @ w_in
    cuts = [A_WIDTH, 2 * A_WIDTH, 3 * A_WIDTH, 3 * A_WIDTH + B_WIDTH,
            3 * A_WIDTH + 2 * B_WIDTH, 3 * A_WIDTH + 3 * B_WIDTH]
    qa, ka, va, qb, kb, vb, gb = jnp.split(h, cuts, axis=-1)
    qa = qa.reshape(bsz, s_len, H_A, 2, DH_QK_A)
    ka = ka.reshape(bsz, s_len, H_A, 2, DH_QK_A)
    va = va.reshape(bsz, s_len, H_A, DH_V_A)
    qb = qb.reshape(bsz, s_len, H_B, DH_B)
    kb = kb.reshape(bsz, s_len, H_B, DH_B)
    vb = vb.reshape(bsz, s_len, H_B, DH_B)
    return qa, ka, va, qb, kb, vb, gb


def diff_lambda(lq1, lk1, lq2, lk2, lam_init):
    f = jnp.float32
    return (jnp.exp(jnp.sum(lq1.astype(f) * lk1.astype(f)))
            - jnp.exp(jnp.sum(lq2.astype(f) * lk2.astype(f))) + lam_init)


def diff_scores(q, k):
    return jnp.einsum('bqhcd,bkhcd->bhcqk', q, k).astype(jnp.float32) * (DH_QK_A ** -0.5)


def diff_weights(s, lam):
    p = jax.nn.softmax(s, axis=-1)
    return p[:, :, 0] - lam * p[:, :, 1]


def diff_attention_prompt(q, k, v, lam):
    bsz, s_len = q.shape[:2]
    n_blocks = s_len // Q_BLOCK
    k_pos = jnp.arange(s_len, dtype=jnp.int32)

    def block(i):
        qs = lax.dynamic_slice_in_dim(q, i * Q_BLOCK, Q_BLOCK, axis=1)
        q_pos = i * Q_BLOCK + jnp.arange(Q_BLOCK, dtype=jnp.int32)
        s = jnp.where(k_pos[None, :] <= q_pos[:, None], diff_scores(qs, k), -jnp.inf)
        a = diff_weights(s, lam).astype(v.dtype)
        return jnp.einsum('bhqk,bkhe->bqhe', a, v)

    out = lax.map(block, jnp.arange(n_blocks, dtype=jnp.int32))
    return jnp.moveaxis(out, 0, 1).reshape(bsz, s_len, H_A, DH_V_A)


def diff_attention_sample(q, k_new, v_new, k_past, v_past, lam):
    n_new = q.shape[1]
    past = k_past.shape[1]
    causal = jnp.arange(n_new)[None, :] <= jnp.arange(n_new)[:, None]
    s = jnp.concatenate([diff_scores(q, k_past),
                         jnp.where(causal, diff_scores(q, k_new), -jnp.inf)], axis=-1)
    a = diff_weights(s, lam).astype(v_new.dtype)
    return (jnp.einsum('bhqk,bkhe->bqhe', a[..., :past], v_past)
            + jnp.einsum('bhqk,bkhe->bqhe', a[..., past:], v_new))


def rotary(x, pos):
    half = x.shape[-1] // 2
    inv = ROPE_BASE ** (-jnp.arange(half, dtype=jnp.float32) / half)
    ang = pos.astype(jnp.float32)[:, None] * inv[None, :]
    cos = jnp.cos(ang)[None, :, None, :].astype(x.dtype)
    sin = jnp.sin(ang)[None, :, None, :].astype(x.dtype)
    x1, x2 = x[..., :half], x[..., half:]
    return jnp.concatenate([x1 * cos - x2 * sin, x2 * cos + x1 * sin], axis=-1)


def retention_qkv(qb, kb, vb, pos):
    q = rotary(qb, pos)
    k = rotary(kb, pos) * (DH_B ** -0.5)
    t = lambda a: jnp.transpose(a, (0, 2, 1, 3))
    return t(q), t(k), t(vb)


def retention_log_decay():
    return jnp.log(1.0 - 2.0 ** (-5.0 - jnp.arange(H_B, dtype=jnp.float32)))


def retention_chunk(s0, q, k, v, log_gamma):
    c = q.shape[2]
    idx = jnp.arange(c, dtype=jnp.float32)
    rel = idx[:, None] - idx[None, :]
    lg = log_gamma[:, None, None]
    decay_mask = jnp.where(rel >= 0, jnp.exp(lg * jnp.maximum(rel, 0.0)), 0.0).astype(q.dtype)
    q_decay = jnp.exp(log_gamma[:, None] * (idx + 1.0)).astype(q.dtype)
    k_decay = jnp.exp(log_gamma[:, None] * (c - 1.0 - idx)).astype(q.dtype)
    chunk_decay = jnp.exp(log_gamma * c).astype(s0.dtype)
    inner = jnp.einsum('bhid,bhjd->bhij', q, k) * decay_mask
    o = (jnp.einsum('bhij,bhje->bhie', inner, v)
         + jnp.einsum('bhid,bhde->bhie', q * q_decay[..., None], s0))
    s_new = (s0 * chunk_decay[:, None, None]
             + jnp.einsum('bhjd,bhje->bhde', k * k_decay[..., None], v))
    return o, s_new


def retention_prompt(q, k, v, log_gamma):
    bsz, nh, s_len, dk = q.shape
    dv = v.shape[-1]
    nc = s_len // RET_CHUNK
    to_chunks = lambda t: jnp.moveaxis(t.reshape(bsz, nh, nc, RET_CHUNK, t.shape[-1]), 2, 0)
    s0 = jnp.zeros((bsz, nh, dk, dv), q.dtype)

    def step(s, qkv):
        o, s = retention_chunk(s, qkv[0], qkv[1], qkv[2], log_gamma)
        return s, o

    s_fin, o = lax.scan(step, s0, (to_chunks(q), to_chunks(k), to_chunks(v)))
    return jnp.moveaxis(o, 0, 2).reshape(bsz, nh, s_len, dv), s_fin


def mixer_output(x, attn, ret, gate, subln_g, lam_init, w_out, g, b):
    bsz, s_len = x.shape[:2]
    a = rms_norm(attn) * subln_g * (1.0 - lam_init)
    r = rms_norm(jnp.transpose(ret, (0, 2, 1, 3)))
    r = r.reshape(bsz, s_len, B_WIDTH) * jax.nn.silu(gate)
    mixed = jnp.concatenate([a.reshape(bsz, s_len, A_WIDTH), r], axis=-1) @ w_out
    return layer_norm(ALPHA * x + mixed, g, b)


def setup_inputs(seed: int = 0) -> dict:
    key = jax.random.key(seed)
    ks = jax.random.split(key, 24)
    nrm = lambda k, shape, scale: jax.random.normal(k, shape, jnp.float32) * scale
    n_pages = PAST_LEN // PAGE_SIZE
    n_used = DEC_BATCH * n_pages
    n_pool = n_used + n_used // 4
    page_table = jax.random.permutation(ks[5], n_pool)[:n_used].reshape(DEC_BATCH, n_pages).astype(jnp.int32)
    return {
        'x_prompt': nrm(ks[0], (BATCH, SEQ, D_MODEL), 1.0),
        'x_sample': nrm(ks[1], (DEC_BATCH, DEC_SEQ, D_MODEL), 1.0),
        'cache_k': nrm(ks[2], (DEPTH, n_pool, PAGE_SIZE, H_A, 2 * DH_QK_A), 1.0),
        'cache_v': nrm(ks[3], (DEPTH, n_pool, PAGE_SIZE, H_A, DH_V_A), 1.0),
        'state_ret': nrm(ks[4], (DEPTH, DEC_BATCH, H_B, DH_B, DH_B), 0.5),
        'page_table': page_table,
        'w_in': nrm(ks[6], (DEPTH, D_MODEL, PROJ_WIDTH), D_MODEL ** -0.5),
        'w_out': nrm(ks[7], (DEPTH, MIX_WIDTH, D_MODEL), BETA * MIX_WIDTH ** -0.5),
        'lambda_q1': nrm(ks[8], (DEPTH, DH_QK_A), 0.1),
        'lambda_k1': nrm(ks[9], (DEPTH, DH_QK_A), 0.1),
        'lambda_q2': nrm(ks[10], (DEPTH, DH_QK_A), 0.1),
        'lambda_k2': nrm(ks[11], (DEPTH, DH_QK_A), 0.1),
        'subln_g': 1.0 + nrm(ks[12], (DEPTH, DH_V_A), 0.02),
        'ffn1_w1': nrm(ks[13], (DEPTH, D_MODEL, D_FF), D_MODEL ** -0.5),
        'ffn1_w3': nrm(ks[14], (DEPTH, D_MODEL, D_FF), D_MODEL ** -0.5),
        'ffn1_w2': nrm(ks[15], (DEPTH, D_FF, D_MODEL), BETA * D_FF ** -0.5),
        'ffn2_w1': nrm(ks[16], (DEPTH, D_MODEL, D_FF), D_MODEL ** -0.5),
        'ffn2_w3': nrm(ks[17], (DEPTH, D_MODEL, D_FF), D_MODEL ** -0.5),
        'ffn2_w2': nrm(ks[18], (DEPTH, D_FF, D_MODEL), BETA * D_FF ** -0.5),
        'ln_g': 1.0 + nrm(ks[19], (DEPTH, 3, D_MODEL), 0.02),
        'ln_b': nrm(ks[20], (DEPTH, 3, D_MODEL), 0.02),
    }


def reference(x_prompt, x_sample, cache_k, cache_v, state_ret, page_table, w_in, w_out,
              lambda_q1, lambda_k1, lambda_q2, lambda_k2, subln_g,
              ffn1_w1, ffn1_w3, ffn1_w2, ffn2_w1, ffn2_w3, ffn2_w2, ln_g, ln_b):
    log_gamma = retention_log_decay()
    bsz_p, s_p = x_prompt.shape[:2]
    bsz_d, s_d = x_sample.shape[:2]
    past = page_table.shape[1] * PAGE_SIZE
    pos_p = jnp.arange(s_p, dtype=jnp.int32)
    pos_d = past + jnp.arange(s_d, dtype=jnp.int32)
    xp, xd = x_prompt, x_sample
    nk_p, nv_p, ns_p, nk_d, nv_d, ns_d = [], [], [], [], [], []
    for l in range(DEPTH):
        lam_init = 0.8 - 0.6 * math.exp(-0.3 * l)
        lam = diff_lambda(lambda_q1[l], lambda_k1[l], lambda_q2[l], lambda_k2[l], lam_init)

        xp = macaron_half(xp, ffn1_w1[l], ffn1_w3[l], ffn1_w2[l], ln_g[l, 0], ln_b[l, 0])
        qa, ka, va, qb, kb, vb, gb = project(xp, w_in[l])
        attn_p = diff_attention_prompt(qa, ka, va, lam)
        q_r, k_r, v_r = retention_qkv(qb, kb, vb, pos_p)
        ret_p, s_fin = retention_prompt(q_r, k_r, v_r, log_gamma)
        xp = mixer_output(xp, attn_p, ret_p, gb, subln_g[l], lam_init, w_out[l], ln_g[l, 1], ln_b[l, 1])
        xp = macaron_half(xp, ffn2_w1[l], ffn2_w3[l], ffn2_w2[l], ln_g[l, 2], ln_b[l, 2])
        nk_p.append(ka.reshape(bsz_p, s_p, H_A, 2 * DH_QK_A))
        nv_p.append(va)
        ns_p.append(s_fin)

        xd = macaron_half(xd, ffn1_w1[l], ffn1_w3[l], ffn1_w2[l], ln_g[l, 0], ln_b[l, 0])
        qa, ka, va, qb, kb, vb, gb = project(xd, w_in[l])
        k_past = cache_k[l, page_table].reshape(bsz_d, past, H_A, 2, DH_QK_A)
        v_past = cache_v[l, page_table].reshape(bsz_d, past, H_A, DH_V_A)
        attn_d = diff_attention_sample(qa, ka, va, k_past, v_past, lam)
        q_r, k_r, v_r = retention_qkv(qb, kb, vb, pos_d)
        ret_d, s_new = retention_chunk(state_ret[l], q_r, k_r, v_r, log_gamma)
        xd = mixer_output(xd, attn_d, ret_d, gb, subln_g[l], lam_init, w_out[l], ln_g[l, 1], ln_b[l, 1])
        xd = macaron_half(xd, ffn2_w1[l], ffn2_w3[l], ffn2_w2[l], ln_g[l, 2], ln_b[l, 2])
        nk_d.append(ka.reshape(bsz_d, s_d, H_A, 2 * DH_QK_A))
        nv_d.append(va)
        ns_d.append(s_new)

    return (xp, xd, jnp.stack(nk_p), jnp.stack(nv_p), jnp.stack(ns_p),
            jnp.stack(nk_d), jnp.stack(nv_d), jnp.stack(ns_d))
```

```python
import functools
import math

import numpy as np
import jax
import jax.numpy as jnp
from jax import lax
from jax.experimental import pallas as pl
from jax.experimental.pallas import tpu as pltpu

F32 = jnp.float32
BF16 = jnp.bfloat16

LN_EPS = 1e-5
ROPE_BASE = 10000.0
LANES = 128
VMEM_LIMIT = 56 * 1024 * 1024


def _params(sem, vmem=VMEM_LIMIT):
    return pltpu.CompilerParams(dimension_semantics=sem, vmem_limit_bytes=vmem)


def _layer_norm(y, g, b):
    mu = jnp.mean(y, axis=-1, keepdims=True)
    d = y - mu
    var = jnp.mean(d * d, axis=-1, keepdims=True)
    return d * lax.rsqrt(var + LN_EPS) * g + b


def _rms_norm(y):
    return y * lax.rsqrt(jnp.mean(y * y, axis=-1, keepdims=True) + LN_EPS)


def _silu(a):
    return a * (1.0 / (1.0 + jnp.exp(-a)))


def _dot_nt(a, b):
    return lax.dot_general(a, b, (((1,), (1,)), ((), ())), preferred_element_type=F32)


def _dot_tn(a, b):
    return lax.dot_general(a, b, (((0,), (0,)), ((), ())), preferred_element_type=F32)


def _dot(a, b):
    return jnp.dot(a, b, preferred_element_type=F32)


def _rope_kernel(inv_ref, cos_ref, sin_ref, *, base, period):
    rows = cos_ref.shape[0]
    row = lax.broadcasted_iota(jnp.int32, cos_ref.shape, 0)
    if period == 1:
        pos = jnp.full(cos_ref.shape, base, jnp.int32)
    else:
        assert period == rows
        pos = row + base
    ang = pos.astype(F32) * inv_ref[...]
    cos_ref[...] = jnp.cos(ang)
    sin_ref[...] = jnp.sin(ang)


def rope_table(rows, half, base, period):
    inv = (ROPE_BASE ** (-jnp.arange(half, dtype=F32) / half)).reshape(1, half)
    return pl.pallas_call(
        functools.partial(_rope_kernel, base=base, period=period),
        out_shape=(jax.ShapeDtypeStruct((rows, half), F32),) * 2,
        name="rope_table",
    )(inv)


def _ffn_kernel(x_ref, w1_ref, w3_ref, w2_ref, g_ref, b_ref, o_ref, xb_ref, acc_ref, *, alpha):
    j = pl.program_id(1)

    @pl.when(j == 0)
    def _():
        xb_ref[...] = x_ref[...].astype(BF16)
        acc_ref[...] = jnp.zeros_like(acc_ref)

    xb = xb_ref[...]
    a = _dot(xb, w1_ref[...])
    b = _dot(xb, w3_ref[...])
    h = (_silu(a) * b).astype(BF16)
    acc_ref[...] += _dot(h, w2_ref[...])

    @pl.when(j == pl.num_programs(1) - 1)
    def _():
        y = alpha * x_ref[...] + 0.5 * acc_ref[...]
        o_ref[...] = _layer_norm(y, g_ref[...], b_ref[...])


def ffn_half(x, w1, w3, w2, g, b, *, alpha, tm, tf):
    m, d = x.shape
    dff = w1.shape[1]
    assert m % tm == 0 and dff % tf == 0
    return pl.pallas_call(
        functools.partial(_ffn_kernel, alpha=alpha),
        grid=(m // tm, dff // tf),
        in_specs=[
            pl.BlockSpec((tm, d), lambda i, j: (i, 0)),
            pl.BlockSpec((d, tf), lambda i, j: (0, j)),
            pl.BlockSpec((d, tf), lambda i, j: (0, j)),
            pl.BlockSpec((tf, d), lambda i, j: (j, 0)),
            pl.BlockSpec((1, d), lambda i, j: (0, 0)),
            pl.BlockSpec((1, d), lambda i, j: (0, 0)),
        ],
        out_specs=pl.BlockSpec((tm, d), lambda i, j: (i, 0)),
        out_shape=jax.ShapeDtypeStruct((m, d), F32),
        scratch_shapes=[pltpu.VMEM((tm, d), BF16), pltpu.VMEM((tm, d), F32)],
        compiler_params=_params(("parallel", "arbitrary")),
        name="ffn_half",
    )(x, w1, w3, w2, g.reshape(1, d), b.reshape(1, d))


N_SEG = 7
SEG_KA, SEG_VA, SEG_QB, SEG_KB, SEG_VB, SEG_G = 1, 2, 3, 4, 5, 6


def _proj_kernel(x_ref, w_ref, cos_ref, sin_ref, h_ref, kf_ref, vf_ref, gf_ref, xb_ref,
                 *, q_scale, kb_scale, dh_b):
    j = pl.program_id(1)

    @pl.when(j == 0)
    def _():
        xb_ref[...] = x_ref[...].astype(BF16)

    acc = _dot(xb_ref[...], w_ref[...])
    hd = h_ref.dtype

    @pl.when(j == 0)
    def _():
        h_ref[...] = (acc * q_scale).astype(hd)

    @pl.when(j == SEG_KA)
    def _():
        kf_ref[...] = acc
        h_ref[...] = acc.astype(hd)

    @pl.when(j == SEG_VA)
    def _():
        vf_ref[...] = acc
        h_ref[...] = acc.astype(hd)

    def rotary(scale):
        cos = cos_ref[...]
        sin = sin_ref[...]
        half = dh_b // 2
        for h in range(acc.shape[1] // dh_b):
            x1 = acc[:, h * dh_b:h * dh_b + half]
            x2 = acc[:, h * dh_b + half:(h + 1) * dh_b]
            h_ref[:, h * dh_b:h * dh_b + half] = ((x1 * cos - x2 * sin) * scale).astype(hd)
            h_ref[:, h * dh_b + half:(h + 1) * dh_b] = ((x2 * cos + x1 * sin) * scale).astype(hd)

    @pl.when(j == SEG_QB)
    def _():
        rotary(1.0)

    @pl.when(j == SEG_KB)
    def _():
        rotary(kb_scale)

    @pl.when(j == SEG_VB)
    def _():
        h_ref[...] = acc.astype(hd)

    @pl.when(j == SEG_G)
    def _():
        gf_ref[...] = acc


def in_proj(x, w_in, cos, sin, *, tm, h_dtype, q_scale, kb_scale, dh_b):
    m, d = x.shape
    seg = w_in.shape[1] // N_SEG
    n_tab = cos.shape[0] // tm
    assert m % tm == 0 and cos.shape[0] % tm == 0
    f32_out = jax.ShapeDtypeStruct((m, seg), F32)
    return pl.pallas_call(
        functools.partial(_proj_kernel, q_scale=q_scale, kb_scale=kb_scale, dh_b=dh_b),
        grid=(m // tm, N_SEG),
        in_specs=[
            pl.BlockSpec((tm, d), lambda i, j: (i, 0)),
            pl.BlockSpec((d, seg), lambda i, j: (0, j)),
            pl.BlockSpec((tm, cos.shape[1]), lambda i, j: (i % n_tab, 0)),
            pl.BlockSpec((tm, cos.shape[1]), lambda i, j: (i % n_tab, 0)),
        ],
        out_specs=[
            pl.BlockSpec((tm, seg), lambda i, j: (i, jnp.minimum(j, SEG_VB))),
            pl.BlockSpec((tm, seg), lambda i, j: (i, 0)),
            pl.BlockSpec((tm, seg), lambda i, j: (i, 0)),
            pl.BlockSpec((tm, seg), lambda i, j: (i, 0)),
        ],
        out_shape=[jax.ShapeDtypeStruct((m, SEG_G * seg), h_dtype), f32_out, f32_out, f32_out],
        scratch_shapes=[pltpu.VMEM((tm, d), BF16)],
        compiler_params=_params(("parallel", "arbitrary")),
        name="in_proj",
    )(x, w_in, cos, sin)


def _diff_lambda(lam4_ref, lam_init):
    v = lam4_ref[...]
    t1 = jnp.sum(v[0:1] * v[1:2], axis=-1, keepdims=True)
    t2 = jnp.sum(v[2:3] * v[3:4], axis=-1, keepdims=True)
    return jnp.exp(t1) - jnp.exp(t2) + lam_init


def _attn_prompt_kernel(lam4_ref, g_ref, q_ref, k_ref, v_ref, o_ref,
                        m1_ref, l1_ref, a1_ref, m2_ref, l2_ref, a2_ref, *, tq, lam_init):
    qi = pl.program_id(2)
    q = q_ref[...]
    lane = lax.broadcasted_iota(jnp.int32, q.shape, 1)
    half = q.shape[1] // 2
    zero = jnp.zeros_like(q)
    q1 = jnp.where(lane < half, q, zero)
    q2 = jnp.where(lane >= half, q, zero)

    for m_ref, l_ref, a_ref in ((m1_ref, l1_ref, a1_ref), (m2_ref, l2_ref, a2_ref)):
        m_ref[...] = jnp.full(m_ref.shape, -jnp.inf, F32)
        l_ref[...] = jnp.zeros(l_ref.shape, F32)
        a_ref[...] = jnp.zeros(a_ref.shape, F32)

    def update(s, v, m_ref, l_ref, a_ref):
        m_old = m_ref[...]
        m_new = jnp.maximum(m_old, jnp.max(s, axis=-1, keepdims=True))
        alpha = jnp.exp(m_old - m_new)
        p = jnp.exp(s - m_new)
        l_ref[...] = alpha * l_ref[...] + jnp.sum(p, axis=-1, keepdims=True)
        a_ref[...] = alpha * a_ref[...] + _dot(p.astype(BF16), v)
        m_ref[...] = m_new

    def block(ki, masked):
        start = pl.multiple_of(ki * tq, tq)
        k = k_ref[pl.ds(start, tq), :]
        v = v_ref[pl.ds(start, tq), :]
        s1 = _dot_nt(q1, k)
        s2 = _dot_nt(q2, k)
        if masked:
            r = lax.broadcasted_iota(jnp.int32, s1.shape, 0)
            c = lax.broadcasted_iota(jnp.int32, s1.shape, 1)
            s1 = jnp.where(c <= r, s1, -jnp.inf)
            s2 = jnp.where(c <= r, s2, -jnp.inf)
        update(s1, v, m1_ref, l1_ref, a1_ref)
        update(s2, v, m2_ref, l2_ref, a2_ref)

    def body(ki, carry):
        block(ki, False)
        return carry

    lax.fori_loop(0, qi, body, 0)
    block(qi, True)

    lam = _diff_lambda(lam4_ref, lam_init)
    o = a1_ref[...] / l1_ref[...] - lam * (a2_ref[...] / l2_ref[...])
    o_ref[...] = (_rms_norm(o) * g_ref[...] * (1.0 - lam_init)).astype(o_ref.dtype)


def attn_prompt(h, lam4, subln_g, *, bsz, s_len, n_heads, dv, tq, lam_init):
    nq = s_len // tq
    assert s_len % tq == 0
    stat = pltpu.VMEM((tq, 1), F32)
    accs = pltpu.VMEM((tq, dv), F32)
    return pl.pallas_call(
        functools.partial(_attn_prompt_kernel, tq=tq, lam_init=lam_init),
        grid=(bsz, n_heads, nq),
        in_specs=[
            pl.BlockSpec(lam4.shape, lambda b, hh, qi: (0, 0)),
            pl.BlockSpec((1, dv), lambda b, hh, qi: (0, 0)),
            pl.BlockSpec((tq, dv), lambda b, hh, qi: (b * nq + qi, hh)),
            pl.BlockSpec((s_len, dv), lambda b, hh, qi: (b, n_heads + hh)),
            pl.BlockSpec((s_len, dv), lambda b, hh, qi: (b, 2 * n_heads + hh)),
        ],
        out_specs=pl.BlockSpec((tq, dv), lambda b, hh, qi: (b * nq + qi, hh)),
        out_shape=jax.ShapeDtypeStruct((bsz * s_len, n_heads * dv), BF16),
        scratch_shapes=[stat, stat, accs, stat, stat, accs],
        compiler_params=_params(("parallel", "parallel", "arbitrary")),
        name="attn_prompt",
    )(lam4, subln_g.reshape(1, dv), h, h, h)


def _attn_sample_kernel(pt_ref, lam4_ref, g_ref, q_ref, kn_ref, vn_ref, *rest, pp, lam_init):
    k_refs = rest[:pp]
    v_refs = rest[pp:2 * pp]
    o_ref, m_ref, l_ref, a_ref = rest[2 * pp:]
    p = pl.program_id(1)
    nh, dv = q_ref.shape[1], q_ref.shape[2]
    half = dv // 2

    q8 = q_ref[0].astype(F32)
    q16 = jnp.concatenate([q8, q8], axis=0)
    row = lax.broadcasted_iota(jnp.int32, q16.shape, 0)
    lane = lax.broadcasted_iota(jnp.int32, q16.shape, 1)
    qm = jnp.where(lane // half == row // nh, q16, 0.0).astype(BF16)

    @pl.when(p == 0)
    def _():
        kn = kn_ref[0].astype(BF16).astype(F32)
        vn = vn_ref[0].astype(BF16).astype(F32)
        kn16 = jnp.concatenate([kn, kn], axis=0)
        m_ref[...] = jnp.sum(qm.astype(F32) * kn16, axis=-1, keepdims=True)
        l_ref[...] = jnp.ones(l_ref.shape, F32)
        a_ref[...] = jnp.concatenate([vn, vn], axis=0)

    for r in range(pp):
        kp = k_refs[r][...]
        n_pos = kp.shape[0]
        kf = kp.reshape(n_pos * nh, dv).astype(BF16)
        vf = v_refs[r][...].reshape(n_pos * nh, dv).astype(BF16)
        s = _dot_nt(qm, kf)
        srow = lax.broadcasted_iota(jnp.int32, s.shape, 0)
        scol = lax.broadcasted_iota(jnp.int32, s.shape, 1)
        s = jnp.where((scol & (nh - 1)) == (srow & (nh - 1)), s, -jnp.inf)
        m_old = m_ref[...]
        m_new = jnp.maximum(m_old, jnp.max(s, axis=-1, keepdims=True))
        alpha = jnp.exp(m_old - m_new)
        pm = jnp.exp(s - m_new)
        l_ref[...] = alpha * l_ref[...] + jnp.sum(pm, axis=-1, keepdims=True)
        a_ref[...] = alpha * a_ref[...] + _dot(pm.astype(BF16), vf)
        m_ref[...] = m_new

    @pl.when(p == pl.num_programs(1) - 1)
    def _():
        lam = _diff_lambda(lam4_ref, lam_init)
        acc = a_ref[...] / l_ref[...]
        o = acc[0:nh] - lam * acc[nh:2 * nh]
        o_ref[0] = _rms_norm(o) * g_ref[...] * (1.0 - lam_init)


def attn_sample(h3, cache_k, cache_v, pt_flat, lam4, subln_g, *, layer, n_pages, pp, lam_init):
    bsz = h3.shape[0]
    _, _, page, nh, dv = cache_k.shape
    assert n_pages % pp == 0 and nh & (nh - 1) == 0

    def page_spec(r):
        return pl.BlockSpec((None, None, page, nh, dv),
                            lambda b, p, pt: (layer, pt[b * n_pages + p * pp + r], 0, 0, 0))

    grid_spec = pltpu.PrefetchScalarGridSpec(
        num_scalar_prefetch=1,
        grid=(bsz, n_pages // pp),
        in_specs=[
            pl.BlockSpec(lam4.shape, lambda b, p, pt: (0, 0)),
            pl.BlockSpec((1, dv), lambda b, p, pt: (0, 0)),
            pl.BlockSpec((1, nh, dv), lambda b, p, pt: (b, 0, 0)),
            pl.BlockSpec((1, nh, dv), lambda b, p, pt: (b, 1, 0)),
            pl.BlockSpec((1, nh, dv), lambda b, p, pt: (b, 2, 0)),
        ] + [page_spec(r) for r in range(pp)] + [page_spec(r) for r in range(pp)],
        out_specs=pl.BlockSpec((1, nh, dv), lambda b, p, pt: (b, 0, 0)),
        scratch_shapes=[pltpu.VMEM((2 * nh, 1), F32), pltpu.VMEM((2 * nh, 1), F32),
                        pltpu.VMEM((2 * nh, dv), F32)],
    )
    return pl.pallas_call(
        functools.partial(_attn_sample_kernel, pp=pp, lam_init=lam_init),
        grid_spec=grid_spec,
        out_shape=jax.ShapeDtypeStruct((bsz, nh, dv), F32),
        compiler_params=_params(("parallel", "arbitrary")),
        name="attn_sample",
    )(pt_flat, lam4, subln_g.reshape(1, dv), h3, h3, h3,
      *([cache_k] * pp), *([cache_v] * pp))


def _log_gamma(h):
    return float(np.log(np.float32(1.0) - np.float32(2.0) ** np.float32(-5.0 - h)))


def _ret_prompt_kernel(q_ref, k_ref, v_ref, gate_ref, r_ref, sfin_ref,
                       st_ref, dm_ref, qd_ref, kd_ref, *, n_heads, dh):
    c = pl.program_id(0)
    bsz, ch = q_ref.shape[0], q_ref.shape[1]

    @pl.when(c == 0)
    def _():
        st_ref[...] = jnp.zeros_like(st_ref)
        i = lax.broadcasted_iota(jnp.int32, (ch, ch), 0)
        jj = lax.broadcasted_iota(jnp.int32, (ch, ch), 1)
        rel = (i - jj).astype(F32)
        idx = lax.broadcasted_iota(jnp.int32, (ch, 1), 0).astype(F32)
        for h in range(n_heads):
            lg = _log_gamma(h)
            dm_ref[h] = jnp.where(rel >= 0, jnp.exp(lg * jnp.maximum(rel, 0.0)), 0.0)
            qd_ref[h] = jnp.exp(lg * (idx + 1.0))
            kd_ref[h] = jnp.exp(lg * (ch - 1.0 - idx))

    for b in range(bsz):
        for h in range(n_heads):
            cols = slice(h * dh, (h + 1) * dh)
            q = q_ref[b, :, cols]
            k = k_ref[b, :, cols]
            v = v_ref[b, :, cols]
            s0 = st_ref[b * n_heads + h]
            inner = (_dot_nt(q, k) * dm_ref[h]).astype(BF16)
            o = _dot(inner, v) + qd_ref[h] * _dot(q, s0.astype(BF16))
            kk = (k.astype(F32) * kd_ref[h]).astype(BF16)
            chunk_decay = float(np.exp(np.float32(_log_gamma(h)) * np.float32(ch)))
            st_ref[b * n_heads + h] = s0 * chunk_decay + _dot_tn(kk, v)
            r_ref[b, :, cols] = (_rms_norm(o) * _silu(gate_ref[b, :, cols])).astype(r_ref.dtype)

    @pl.when(c == pl.num_programs(0) - 1)
    def _():
        for b in range(bsz):
            for h in range(n_heads):
                sfin_ref[b, h] = st_ref[b * n_heads + h]


def ret_prompt(h3, gate3, *, n_heads, dh, chunk):
    bsz, s_len, _ = h3.shape
    width = n_heads * dh
    assert s_len % chunk == 0

    def seg_spec(sg):
        return pl.BlockSpec((bsz, chunk, width), lambda c: (0, c, sg))

    return pl.pallas_call(
        functools.partial(_ret_prompt_kernel, n_heads=n_heads, dh=dh),
        grid=(s_len // chunk,),
        in_specs=[seg_spec(SEG_QB), seg_spec(SEG_KB), seg_spec(SEG_VB), seg_spec(0)],
        out_specs=[
            pl.BlockSpec((bsz, chunk, width), lambda c: (0, c, 0)),
            pl.BlockSpec((bsz, n_heads, dh, dh), lambda c: (0, 0, 0, 0)),
        ],
        out_shape=[jax.ShapeDtypeStruct((bsz, s_len, width), BF16),
                   jax.ShapeDtypeStruct((bsz, n_heads, dh, dh), F32)],
        scratch_shapes=[
            pltpu.VMEM((bsz * n_heads, dh, dh), F32),
            pltpu.VMEM((n_heads, chunk, chunk), F32),
            pltpu.VMEM((n_heads, chunk, 1), F32),
            pltpu.VMEM((n_heads, chunk, 1), F32),
        ],
        compiler_params=_params(("arbitrary",)),
        name="ret_prompt",
    )(h3, h3, h3, gate3)


def _ret_sample_kernel(q_ref, k_ref, v_ref, gate_ref, st_ref, *rest, n_heads, dh, aliased):
    if aliased:
        rest = rest[1:]
    r_ref, snew_ref = rest
    rows = q_ref.shape[0]
    row = lax.broadcasted_iota(jnp.int32, (rows, dh), 0)

    def body(bi, o_acc):
        out = []
        for h in range(n_heads):
            cols = slice(h * dh, (h + 1) * dh)
            sel = row == bi
            qb = jnp.where(sel, q_ref[:, cols], 0.0).astype(BF16)
            kb = jnp.where(sel, k_ref[:, cols], 0.0).astype(BF16)
            vb = v_ref[:, cols].astype(BF16)
            gamma = float(np.exp(np.float32(_log_gamma(h))))
            s_new = st_ref[bi, h] * gamma + _dot_tn(kb, vb)
            snew_ref[bi, h] = s_new
            out.append(o_acc[h] + _dot(qb, s_new.astype(BF16)))
        return tuple(out)

    o = lax.fori_loop(0, rows, body, tuple(jnp.zeros((rows, dh), F32) for _ in range(n_heads)))
    for h in range(n_heads):
        cols = slice(h * dh, (h + 1) * dh)
        r_ref[:, cols] = _rms_norm(o[h]) * _silu(gate_ref[:, cols])


def ret_sample(h2, gate, state, prev_out, *, layer, n_heads, dh, rows):
    bsz = h2.shape[0]
    width = n_heads * dh
    depth = state.shape[0]
    assert bsz % rows == 0
    aliased = prev_out is not None

    def seg_spec(sg):
        return pl.BlockSpec((rows, width), lambda i: (i, sg))

    st_spec = pl.BlockSpec((None, rows, n_heads, dh, dh), lambda i: (layer, i, 0, 0, 0))
    in_specs = [seg_spec(SEG_QB), seg_spec(SEG_KB), seg_spec(SEG_VB), seg_spec(0), st_spec]
    args = [h2, h2, h2, gate, state]
    if aliased:
        in_specs.append(pl.BlockSpec(memory_space=pl.ANY))
        args.append(prev_out)
    return pl.pallas_call(
        functools.partial(_ret_sample_kernel, n_heads=n_heads, dh=dh, aliased=aliased),
        grid=(bsz // rows,),
        in_specs=in_specs,
        out_specs=[pl.BlockSpec((rows, width), lambda i: (i, 0)), st_spec],
        out_shape=[jax.ShapeDtypeStruct((bsz, width), F32),
                   jax.ShapeDtypeStruct((depth, bsz, n_heads, dh, dh), F32)],
        input_output_aliases={5: 1} if aliased else {},
        compiler_params=_params(("parallel",)),
        name="ret_sample",
    )(*args)


def _out_proj_kernel(a_ref, r_ref, x_ref, w_ref, g_ref, b_ref, o_ref, *, alpha):
    wa = a_ref.shape[1]
    mixed = (_dot(a_ref[...].astype(BF16), w_ref[0:wa, :])
             + _dot(r_ref[...].astype(BF16), w_ref[wa:, :]))
    o_ref[...] = _layer_norm(alpha * x_ref[...] + mixed, g_ref[...], b_ref[...])


def out_proj(a, r, x, w_out, g, b, *, alpha, tm):
    m, d = x.shape
    assert m % tm == 0
    return pl.pallas_call(
        functools.partial(_out_proj_kernel, alpha=alpha),
        grid=(m // tm,),
        in_specs=[
            pl.BlockSpec((tm, a.shape[1]), lambda i: (i, 0)),
            pl.BlockSpec((tm, r.shape[1]), lambda i: (i, 0)),
            pl.BlockSpec((tm, d), lambda i: (i, 0)),
            pl.BlockSpec(w_out.shape, lambda i: (0, 0)),
            pl.BlockSpec((1, d), lambda i: (0, 0)),
            pl.BlockSpec((1, d), lambda i: (0, 0)),
        ],
        out_specs=pl.BlockSpec((tm, d), lambda i: (i, 0)),
        out_shape=jax.ShapeDtypeStruct((m, d), F32),
        compiler_params=_params(("parallel",)),
        name="out_proj",
    )(a, r, x, w_out, g.reshape(1, d), b.reshape(1, d))


def _row_tile(m, cap):
    t = min(m, cap)
    while m % t:
        t //= 2
    return t


def kernel(x_prompt, x_sample, cache_k, cache_v, state_ret, page_table, w_in, w_out,
           lambda_q1, lambda_k1, lambda_q2, lambda_k2, subln_g,
           ffn1_w1, ffn1_w3, ffn1_w2, ffn2_w1, ffn2_w3, ffn2_w2, ln_g, ln_b):
    bsz_p, s_p, d = x_prompt.shape
    bsz_d, s_d, _ = x_sample.shape
    assert s_d == 1, "one new token per sample sequence"
    depth, _, page, n_ha, dv = cache_k.shape
    n_hb, dh_b = state_ret.shape[2], state_ret.shape[3]
    n_pages = page_table.shape[1]
    past = n_pages * page
    dqk = dv // 2
    a_width = n_ha * dv
    b_width = n_hb * dh_b
    assert w_in.shape[2] == N_SEG * a_width and a_width == b_width
    dff = ffn1_w1.shape[2]
    alpha = (2 * depth) ** 0.25

    m_p = bsz_p * s_p
    m_d = bsz_d * s_d
    xp = x_prompt.reshape(m_p, d)
    xd = x_sample.reshape(m_d, d)

    cos_p, sin_p = rope_table(s_p, dh_b // 2, 0, s_p)
    cos_d, sin_d = rope_table(m_d, dh_b // 2, past, 1)

    wb = lambda w: w.astype(BF16)
    w_in_b, w_out_b = wb(w_in), wb(w_out)
    f1 = (wb(ffn1_w1), wb(ffn1_w3), wb(ffn1_w2))
    f2 = (wb(ffn2_w1), wb(ffn2_w3), wb(ffn2_w2))
    pt_flat = page_table.reshape(-1)

    tf = _row_tile(dff, 512)
    tm_ffn_p, tm_ffn_d = _row_tile(m_p, 512), _row_tile(m_d, 512)
    tm_proj_p, tm_proj_d = _row_tile(s_p, 512), _row_tile(m_d, 512)
    tq = _row_tile(s_p, 512)
    chunk = _row_tile(s_p, 256)
    ret_rows = 8

    proj_kw = dict(q_scale=dqk ** -0.5, kb_scale=dh_b ** -0.5, dh_b=dh_b)
    nk_p, nv_p, ns_p, nk_d, nv_d = [], [], [], [], []
    ns_d = None
    for l in range(depth):
        lam_init = 0.8 - 0.6 * math.exp(-0.3 * l)
        lam4 = jnp.stack([lambda_q1[l], lambda_k1[l], lambda_q2[l], lambda_k2[l]]).astype(F32)
        ffn1 = tuple(w[l] for w in f1)
        ffn2 = tuple(w[l] for w in f2)

        xp = ffn_half(xp, *ffn1, ln_g[l, 0], ln_b[l, 0], alpha=alpha, tm=tm_ffn_p, tf=tf)
        hp, kf, vf, gf = in_proj(xp, w_in_b[l], cos_p, sin_p, tm=tm_proj_p, h_dtype=BF16, **proj_kw)
        attn = attn_prompt(hp, lam4, subln_g[l], bsz=bsz_p, s_len=s_p, n_heads=n_ha, dv=dv,
                           tq=tq, lam_init=lam_init)
        ret, s_fin = ret_prompt(hp.reshape(bsz_p, s_p, -1), gf.reshape(bsz_p, s_p, -1),
                                n_heads=n_hb, dh=dh_b, chunk=chunk)
        xp = out_proj(attn, ret.reshape(m_p, b_width), xp, w_out_b[l], ln_g[l, 1], ln_b[l, 1],
                      alpha=alpha, tm=tm_ffn_p)
        xp = ffn_half(xp, *ffn2, ln_g[l, 2], ln_b[l, 2], alpha=alpha, tm=tm_ffn_p, tf=tf)
        nk_p.append(kf.reshape(bsz_p, s_p, n_ha, dv))
        nv_p.append(vf.reshape(bsz_p, s_p, n_ha, dv))
        ns_p.append(s_fin)

        xd = ffn_half(xd, *ffn1, ln_g[l, 0], ln_b[l, 0], alpha=alpha, tm=tm_ffn_d, tf=tf)
        hd, kf, vf, gf = in_proj(xd, w_in_b[l], cos_d, sin_d, tm=tm_proj_d, h_dtype=F32, **proj_kw)
        attn = attn_sample(hd.reshape(m_d, 6 * n_ha, dv), cache_k, cache_v, pt_flat, lam4, subln_g[l],
                           layer=l, n_pages=n_pages, pp=4, lam_init=lam_init)
        ret, ns_d = ret_sample(hd, gf, state_ret, ns_d, layer=l, n_heads=n_hb, dh=dh_b, rows=ret_rows)
        xd = out_proj(attn.reshape(m_d, a_width), ret, xd, w_out_b[l], ln_g[l, 1], ln_b[l, 1],
                      alpha=alpha, tm=tm_ffn_d)
        xd = ffn_half(xd, *ffn2, ln_g[l, 2], ln_b[l, 2], alpha=alpha, tm=tm_ffn_d, tf=tf)
        nk_d.append(kf.reshape(bsz_d, s_d, n_ha, dv))
        nv_d.append(vf.reshape(bsz_d, s_d, n_ha, dv))

    return (xp.reshape(bsz_p, s_p, d), xd.reshape(bsz_d, s_d, d),
            jnp.stack(nk_p), jnp.stack(nv_p), jnp.stack(ns_p),
            jnp.stack(nk_d), jnp.stack(nv_d), ns_d)
```

```python
import functools
import math

import numpy as np
import jax
import jax.numpy as jnp
from jax import lax
from jax.experimental import pallas as pl
from jax.experimental.pallas import tpu as pltpu

F32 = jnp.float32
BF16 = jnp.bfloat16

LN_EPS = 1e-5
ROPE_BASE = 10000.0
LANES = 128
VMEM_LIMIT = 56 * 1024 * 1024


def _params(sem, vmem=VMEM_LIMIT):
    return pltpu.CompilerParams(dimension_semantics=sem, vmem_limit_bytes=vmem)


def _layer_norm(y, g, b):
    mu = jnp.mean(y, axis=-1, keepdims=True)
    d = y - mu
    var = jnp.mean(d * d, axis=-1, keepdims=True)
    return d * lax.rsqrt(var + LN_EPS) * g + b


def _rms_norm(y):
    return y * lax.rsqrt(jnp.mean(y * y, axis=-1, keepdims=True) + LN_EPS)


def _silu(a):
    return a * (1.0 / (1.0 + jnp.exp(-a)))


def _dot_nt(a, b):
    return lax.dot_general(a, b, (((1,), (1,)), ((), ())), preferred_element_type=F32)


def _dot_tn(a, b):
    return lax.dot_general(a, b, (((0,), (0,)), ((), ())), preferred_element_type=F32)


def _dot(a, b):
    return jnp.dot(a, b, preferred_element_type=F32)


def _rope_kernel(inv_ref, cos_ref, sin_ref, *, base, period):
    rows = cos_ref.shape[0]
    row = lax.broadcasted_iota(jnp.int32, cos_ref.shape, 0)
    if period == 1:
        pos = jnp.full(cos_ref.shape, base, jnp.int32)
    else:
        assert period == rows
        pos = row + base
    ang = pos.astype(F32) * inv_ref[...]
    cos_ref[...] = jnp.cos(ang)
    sin_ref[...] = jnp.sin(ang)


def rope_table(rows, half, base, period):
    inv = (ROPE_BASE ** (-jnp.arange(half, dtype=F32) / half)).reshape(1, half)
    return pl.pallas_call(
        functools.partial(_rope_kernel, base=base, period=period),
        out_shape=(jax.ShapeDtypeStruct((rows, half), F32),) * 2,
        name="rope_table",
    )(inv)


def _layer_spec(w, layer, block, index_map):
    if w.ndim == 2:
        return pl.BlockSpec(block, index_map)
    return pl.BlockSpec((None,) + block, lambda *g: (layer,) + index_map(*g))


def _ffn_kernel(x_ref, w1_ref, w3_ref, w2_ref, g_ref, b_ref, o_ref, *rest, alpha):
    xb_ref = rest[-1]
    j = pl.program_id(1)

    @pl.when(j == 0)
    def _():
        xb_ref[...] = x_ref[...].astype(BF16)
        o_ref[...] = jnp.zeros_like(o_ref)

    w1, w3, w2 = (r[...].astype(BF16) for r in (w1_ref, w3_ref, w2_ref))
    for dst, w in zip(rest[:-1], (w1, w3, w2)):
        dst[...] = w

    xb = xb_ref[...]
    a = _dot(xb, w1)
    b = _dot(xb, w3)
    h = (_silu(a) * b).astype(BF16)
    o_ref[...] += _dot(h, w2)

    @pl.when(j == pl.num_programs(1) - 1)
    def _():
        y = alpha * x_ref[...] + 0.5 * o_ref[...]
        o_ref[...] = _layer_norm(y, g_ref[...], b_ref[...])


def ffn_half(x, w1, w3, w2, g, b, *, layer, alpha, tm, tf):
    m, d = x.shape
    dff = w1.shape[-1]
    assert m % tm == 0 and dff % tf == 0
    up = ((d, tf), lambda i, j: (0, j))
    down = ((tf, d), lambda i, j: (j, 0))
    out_specs = [pl.BlockSpec((tm, d), lambda i, j: (i, 0))]
    out_shape = [jax.ShapeDtypeStruct((m, d), F32)]
    if w1.dtype != BF16:
        assert m == tm, "bf16 weight tiles are emitted once, by a single row tile"
        out_specs += [pl.BlockSpec(*up), pl.BlockSpec(*up), pl.BlockSpec(*down)]
        out_shape += [jax.ShapeDtypeStruct(w.shape[-2:], BF16) for w in (w1, w3, w2)]
    return pl.pallas_call(
        functools.partial(_ffn_kernel, alpha=alpha),
        grid=(m // tm, dff // tf),
        in_specs=[
            pl.BlockSpec((tm, d), lambda i, j: (i, 0), pipeline_mode=pl.Buffered(1)),
            _layer_spec(w1, layer, *up),
            _layer_spec(w3, layer, *up),
            _layer_spec(w2, layer, *down),
            pl.BlockSpec((1, d), lambda i, j: (0, 0)),
            pl.BlockSpec((1, d), lambda i, j: (0, 0)),
        ],
        out_specs=out_specs,
        out_shape=out_shape,
        scratch_shapes=[pltpu.VMEM((tm, d), BF16)],
        compiler_params=_params(("parallel", "arbitrary")),
        name="ffn_half",
    )(x, w1, w3, w2, g.reshape(1, d), b.reshape(1, d))


N_SEG = 7
SEG_KA, SEG_VA, SEG_QB, SEG_KB, SEG_VB, SEG_G = 1, 2, 3, 4, 5, 6


def _proj_kernel(x_ref, w_ref, cos_ref, sin_ref, *rest, n_alias, q_scale, kb_scale, dh_b):
    h_ref, kf_ref, vf_ref, gf_ref = rest[n_alias:n_alias + 4]
    xb_ref = rest[-1]
    j = pl.program_id(1)

    @pl.when(j == 0)
    def _():
        xb_ref[...] = x_ref[...].astype(BF16)

    w = w_ref[...].astype(BF16)
    for dst in rest[n_alias + 4:-1]:
        dst[...] = w
    acc = _dot(xb_ref[...], w)
    hd = h_ref.dtype

    @pl.when(j == 0)
    def _():
        h_ref[...] = (acc * q_scale).astype(hd)

    @pl.when(j == SEG_KA)
    def _():
        kf_ref[...] = acc
        h_ref[...] = acc.astype(hd)

    @pl.when(j == SEG_VA)
    def _():
        vf_ref[...] = acc
        h_ref[...] = acc.astype(hd)

    def rotary(scale):
        cos = cos_ref[...]
        sin = sin_ref[...]
        half = dh_b // 2
        for h in range(acc.shape[1] // dh_b):
            x1 = acc[:, h * dh_b:h * dh_b + half]
            x2 = acc[:, h * dh_b + half:(h + 1) * dh_b]
            h_ref[:, h * dh_b:h * dh_b + half] = ((x1 * cos - x2 * sin) * scale).astype(hd)
            h_ref[:, h * dh_b + half:(h + 1) * dh_b] = ((x2 * cos + x1 * sin) * scale).astype(hd)

    @pl.when(j == SEG_QB)
    def _():
        rotary(1.0)

    @pl.when(j == SEG_KB)
    def _():
        rotary(kb_scale)

    @pl.when(j == SEG_VB)
    def _():
        h_ref[...] = acc.astype(hd)

    @pl.when(j == SEG_G)
    def _():
        gf_ref[...] = acc


def in_proj(x, w_in, cos, sin, prev_kv, *, layer, depth, tm, h_dtype, q_scale, kb_scale, dh_b):
    m, d = x.shape
    seg = w_in.shape[-1] // N_SEG
    n_tab = cos.shape[0] // tm
    assert m % tm == 0 and cos.shape[0] % tm == 0
    kv_out = jax.ShapeDtypeStruct((depth, m, seg), F32)
    kv_spec = pl.BlockSpec((None, tm, seg), lambda i, j: (layer, i, 0))
    w_block = ((d, seg), lambda i, j: (0, j))
    in_specs = [
        pl.BlockSpec((tm, d), lambda i, j: (i, 0)),
        _layer_spec(w_in, layer, *w_block),
        pl.BlockSpec((tm, cos.shape[1]), lambda i, j: (i % n_tab, 0)),
        pl.BlockSpec((tm, cos.shape[1]), lambda i, j: (i % n_tab, 0)),
    ]
    args = [x, w_in, cos, sin]
    aliases = {}
    if prev_kv is not None:
        in_specs += [pl.BlockSpec(memory_space=pl.ANY)] * 2
        args += list(prev_kv)
        aliases = {4: 1, 5: 2}
    out_specs = [
        pl.BlockSpec((tm, seg), lambda i, j: (i, jnp.minimum(j, SEG_VB))),
        kv_spec,
        kv_spec,
        pl.BlockSpec((tm, seg), lambda i, j: (i, 0)),
    ]
    out_shape = [jax.ShapeDtypeStruct((m, SEG_G * seg), h_dtype), kv_out, kv_out,
                 jax.ShapeDtypeStruct((m, seg), F32)]
    if w_in.dtype != BF16:
        assert m == tm, "bf16 weight tiles are emitted once, by a single row tile"
        out_specs.append(pl.BlockSpec(*w_block))
        out_shape.append(jax.ShapeDtypeStruct(w_in.shape[-2:], BF16))
    return pl.pallas_call(
        functools.partial(_proj_kernel, n_alias=len(aliases), q_scale=q_scale, kb_scale=kb_scale,
                          dh_b=dh_b),
        grid=(m // tm, N_SEG),
        in_specs=in_specs,
        out_specs=out_specs,
        out_shape=out_shape,
        input_output_aliases=aliases,
        scratch_shapes=[pltpu.VMEM((tm, d), BF16)],
        compiler_params=_params(("parallel", "arbitrary")),
        name="in_proj",
    )(*args)


def _diff_lambda(lam4_ref, lam_init):
    v = lam4_ref[...]
    t1 = jnp.sum(v[0:1] * v[1:2], axis=-1, keepdims=True)
    t2 = jnp.sum(v[2:3] * v[3:4], axis=-1, keepdims=True)
    return jnp.exp(t1) - jnp.exp(t2) + lam_init


def _attn_prompt_kernel(lam4_ref, g_ref, q_ref, k_ref, v_ref, o_ref,
                        vt_ref, m1_ref, l1_ref, a1_ref, m2_ref, l2_ref, a2_ref, *, tq, lam_init):
    qi = pl.program_id(2)

    @pl.when(qi == 0)
    def _():
        for j in range(vt_ref.shape[0]):
            vt_ref[j] = v_ref[j * tq:(j + 1) * tq, :].astype(F32).T.astype(BF16)

    q = q_ref[...]
    lane = lax.broadcasted_iota(jnp.int32, q.shape, 1)
    half = q.shape[1] // 2
    zero = jnp.zeros_like(q)
    q1 = jnp.where(lane < half, q, zero)
    q2 = jnp.where(lane >= half, q, zero)

    for m_ref, l_ref, a_ref in ((m1_ref, l1_ref, a1_ref), (m2_ref, l2_ref, a2_ref)):
        m_ref[...] = jnp.full(m_ref.shape, -jnp.inf, F32)
        l_ref[...] = jnp.zeros(l_ref.shape, F32)
        a_ref[...] = jnp.zeros(a_ref.shape, F32)

    def update(s, vt, m_ref, l_ref, a_ref):
        m_old = m_ref[...]
        m_new = jnp.maximum(m_old, jnp.max(s, axis=0, keepdims=True))
        alpha = jnp.exp(m_old - m_new)
        p = jnp.exp(s - m_new)
        l_ref[...] = alpha * l_ref[...] + jnp.sum(p, axis=0, keepdims=True)
        a_ref[...] = alpha * a_ref[...] + _dot(vt, p.astype(BF16))
        m_ref[...] = m_new

    def block(ki, masked):
        start = pl.multiple_of(ki * tq, tq)
        k = k_ref[pl.ds(start, tq), :]
        vt = vt_ref[ki]
        s1 = _dot_nt(k, q1)
        s2 = _dot_nt(k, q2)
        if masked:
            r = lax.broadcasted_iota(jnp.int32, s1.shape, 0)
            c = lax.broadcasted_iota(jnp.int32, s1.shape, 1)
            s1 = jnp.where(r <= c, s1, -jnp.inf)
            s2 = jnp.where(r <= c, s2, -jnp.inf)
        update(s1, vt, m1_ref, l1_ref, a1_ref)
        update(s2, vt, m2_ref, l2_ref, a2_ref)

    def body(ki, carry):
        block(ki, False)
        return carry

    lax.fori_loop(0, qi, body, 0)
    block(qi, True)

    lam = _diff_lambda(lam4_ref, lam_init)
    ot = a1_ref[...] / l1_ref[...] - lam * (a2_ref[...] / l2_ref[...])
    ot = ot * lax.rsqrt(jnp.mean(ot * ot, axis=0, keepdims=True) + LN_EPS)
    o_ref[...] = (ot.T * g_ref[...] * (1.0 - lam_init)).astype(o_ref.dtype)


def attn_prompt(h, lam4, subln_g, *, bsz, s_len, n_heads, dv, tq, lam_init):
    nq = s_len // tq
    assert s_len % tq == 0
    stat = pltpu.VMEM((1, tq), F32)
    accs = pltpu.VMEM((dv, tq), F32)
    return pl.pallas_call(
        functools.partial(_attn_prompt_kernel, tq=tq, lam_init=lam_init),
        grid=(bsz, n_heads, nq),
        in_specs=[
            pl.BlockSpec(lam4.shape, lambda b, hh, qi: (0, 0)),
            pl.BlockSpec((1, dv), lambda b, hh, qi: (0, 0)),
            pl.BlockSpec((tq, dv), lambda b, hh, qi: (b * nq + qi, hh)),
            pl.BlockSpec((s_len, dv), lambda b, hh, qi: (b, n_heads + hh)),
            pl.BlockSpec((s_len, dv), lambda b, hh, qi: (b, 2 * n_heads + hh)),
        ],
        out_specs=pl.BlockSpec((tq, dv), lambda b, hh, qi: (b * nq + qi, hh)),
        out_shape=jax.ShapeDtypeStruct((bsz * s_len, n_heads * dv), BF16),
        scratch_shapes=[pltpu.VMEM((nq, dv, tq), BF16), stat, stat, accs, stat, stat, accs],
        compiler_params=_params(("parallel", "parallel", "arbitrary")),
        name="attn_prompt",
    )(lam4, subln_g.reshape(1, dv), h, h, h)


def _attn_sample_kernel(pt_ref, lam4_ref, g_ref, q_ref, kn_ref, vn_ref, *rest, pp, lam_init):
    k_refs = rest[:pp]
    v_refs = rest[pp:2 * pp]
    o_ref = rest[2 * pp]
    nh, dv = q_ref.shape[1], q_ref.shape[2]
    half = dv // 2

    q8 = q_ref[0].astype(F32)
    q16 = jnp.concatenate([q8, q8], axis=0)
    row = lax.broadcasted_iota(jnp.int32, q16.shape, 0)
    lane = lax.broadcasted_iota(jnp.int32, q16.shape, 1)
    qm = jnp.where(lane // half == row // nh, q16, 0.0).astype(BF16)

    kn = kn_ref[0].astype(BF16).astype(F32)
    vn = vn_ref[0].astype(BF16).astype(F32)
    s_self = jnp.sum(qm.astype(F32) * jnp.concatenate([kn, kn], axis=0), axis=-1, keepdims=True)

    n_col = k_refs[0].shape[0] * nh
    srow = lax.broadcasted_iota(jnp.int32, (2 * nh, n_col), 0)
    scol = lax.broadcasted_iota(jnp.int32, (2 * nh, n_col), 1)
    valid = (scol & (nh - 1)) == (srow & (nh - 1))

    scores = []
    for r in range(pp):
        kf = k_refs[r][...].reshape(n_col, dv).astype(BF16)
        scores.append(jnp.where(valid, _dot_nt(qm, kf), -jnp.inf))
    mx = functools.reduce(jnp.maximum, scores)
    m = jnp.maximum(s_self, jnp.max(mx, axis=-1, keepdims=True))
    w_self = jnp.exp(s_self - m)
    probs = [jnp.exp(s - m) for s in scores]
    l = w_self + jnp.sum(functools.reduce(jnp.add, probs), axis=-1, keepdims=True)
    acc = w_self * jnp.concatenate([vn, vn], axis=0)
    for r in range(pp):
        vf = v_refs[r][...].reshape(n_col, dv).astype(BF16)
        acc = acc + _dot(probs[r].astype(BF16), vf)

    lam = _diff_lambda(lam4_ref, lam_init)
    acc = acc / l
    o = acc[0:nh] - lam * acc[nh:2 * nh]
    o_ref[0] = _rms_norm(o) * g_ref[...] * (1.0 - lam_init)


def attn_sample(h3, cache_k, cache_v, pt_flat, lam4, subln_g, *, layer, n_pages, lam_init):
    bsz = h3.shape[0]
    _, _, page, nh, dv = cache_k.shape
    assert nh & (nh - 1) == 0
    pp = n_pages

    def page_spec(r):
        return pl.BlockSpec((None, None, page, nh, dv),
                            lambda b, pt: (layer, pt[b * n_pages + r], 0, 0, 0))

    grid_spec = pltpu.PrefetchScalarGridSpec(
        num_scalar_prefetch=1,
        grid=(bsz,),
        in_specs=[
            pl.BlockSpec(lam4.shape, lambda b, pt: (0, 0)),
            pl.BlockSpec((1, dv), lambda b, pt: (0, 0)),
            pl.BlockSpec((1, nh, dv), lambda b, pt: (b, 0, 0)),
            pl.BlockSpec((1, nh, dv), lambda b, pt: (b, 1, 0)),
            pl.BlockSpec((1, nh, dv), lambda b, pt: (b, 2, 0)),
        ] + [page_spec(r) for r in range(pp)] + [page_spec(r) for r in range(pp)],
        out_specs=pl.BlockSpec((1, nh, dv), lambda b, pt: (b, 0, 0)),
    )
    return pl.pallas_call(
        functools.partial(_attn_sample_kernel, pp=pp, lam_init=lam_init),
        grid_spec=grid_spec,
        out_shape=jax.ShapeDtypeStruct((bsz, nh, dv), F32),
        compiler_params=_params(("parallel",)),
        name="attn_sample",
    )(pt_flat, lam4, subln_g.reshape(1, dv), h3, h3, h3,
      *([cache_k] * pp), *([cache_v] * pp))


def _log_gamma(h):
    return float(np.log(np.float32(1.0) - np.float32(2.0) ** np.float32(-5.0 - h)))


def _ret_prompt_kernel(q_ref, k_ref, v_ref, gate_ref, r_ref, sfin_ref,
                       st_ref, dm_ref, qd_ref, kd_ref, *, n_heads, dh):
    c = pl.program_id(0)
    bsz, ch = q_ref.shape[0], q_ref.shape[1]

    @pl.when(c == 0)
    def _():
        st_ref[...] = jnp.zeros_like(st_ref)
        i = lax.broadcasted_iota(jnp.int32, (ch, ch), 0)
        jj = lax.broadcasted_iota(jnp.int32, (ch, ch), 1)
        rel = (i - jj).astype(F32)
        idx = lax.broadcasted_iota(jnp.int32, (ch, 1), 0).astype(F32)
        for h in range(n_heads):
            lg = _log_gamma(h)
            dm_ref[h] = jnp.where(rel >= 0, jnp.exp(lg * jnp.maximum(rel, 0.0)), 0.0)
            qd_ref[h] = jnp.exp(lg * (idx + 1.0))
            kd_ref[h] = jnp.exp(lg * (ch - 1.0 - idx))

    for b in range(bsz):
        for h in range(n_heads):
            cols = slice(h * dh, (h + 1) * dh)
            q = q_ref[b, :, cols]
            k = k_ref[b, :, cols]
            v = v_ref[b, :, cols]
            s0 = st_ref[b * n_heads + h]
            inner = (_dot_nt(q, k) * dm_ref[h]).astype(BF16)
            o = _dot(inner, v) + qd_ref[h] * _dot(q, s0.astype(BF16))
            kk = (k.astype(F32) * kd_ref[h]).astype(BF16)
            chunk_decay = float(np.exp(np.float32(_log_gamma(h)) * np.float32(ch)))
            st_ref[b * n_heads + h] = s0 * chunk_decay + _dot_tn(kk, v)
            r_ref[b, :, cols] = (_rms_norm(o) * _silu(gate_ref[b, :, cols])).astype(r_ref.dtype)

    @pl.when(c == pl.num_programs(0) - 1)
    def _():
        for b in range(bsz):
            for h in range(n_heads):
                sfin_ref[b, h] = st_ref[b * n_heads + h]


def ret_prompt(h3, gate3, *, n_heads, dh, chunk):
    bsz, s_len, _ = h3.shape
    width = n_heads * dh
    assert s_len % chunk == 0

    def seg_spec(sg):
        return pl.BlockSpec((bsz, chunk, width), lambda c: (0, c, sg))

    return pl.pallas_call(
        functools.partial(_ret_prompt_kernel, n_heads=n_heads, dh=dh),
        grid=(s_len // chunk,),
        in_specs=[seg_spec(SEG_QB), seg_spec(SEG_KB), seg_spec(SEG_VB), seg_spec(0)],
        out_specs=[
            pl.BlockSpec((bsz, chunk, width), lambda c: (0, c, 0)),
            pl.BlockSpec((bsz, n_heads, dh, dh), lambda c: (0, 0, 0, 0)),
        ],
        out_shape=[jax.ShapeDtypeStruct((bsz, s_len, width), BF16),
                   jax.ShapeDtypeStruct((bsz, n_heads, dh, dh), F32)],
        scratch_shapes=[
            pltpu.VMEM((bsz * n_heads, dh, dh), F32),
            pltpu.VMEM((n_heads, chunk, chunk), F32),
            pltpu.VMEM((n_heads, chunk, 1), F32),
            pltpu.VMEM((n_heads, chunk, 1), F32),
        ],
        compiler_params=_params(("arbitrary",)),
        name="ret_prompt",
    )(h3, h3, h3, gate3)


def _ret_sample_kernel(q_ref, k_ref, v_ref, gate_ref, st_ref, *rest, n_heads, dh, aliased):
    if aliased:
        rest = rest[1:]
    r_ref, snew_ref = rest
    rows = q_ref.shape[0]
    row = lax.broadcasted_iota(jnp.int32, (rows, dh), 0)

    def body(bi, o_acc):
        out = []
        for h in range(n_heads):
            cols = slice(h * dh, (h + 1) * dh)
            sel = row == bi
            qb = jnp.where(sel, q_ref[:, cols], 0.0).astype(BF16)
            kb = jnp.where(sel, k_ref[:, cols], 0.0).astype(BF16)
            vb = v_ref[:, cols].astype(BF16)
            gamma = float(np.exp(np.float32(_log_gamma(h))))
            s_new = st_ref[bi, h] * gamma + _dot_tn(kb, vb)
            snew_ref[bi, h] = s_new
            out.append(o_acc[h] + _dot(qb, s_new.astype(BF16)))
        return tuple(out)

    o = lax.fori_loop(0, rows, body, tuple(jnp.zeros((rows, dh), F32) for _ in range(n_heads)))
    for h in range(n_heads):
        cols = slice(h * dh, (h + 1) * dh)
        r_ref[:, cols] = _rms_norm(o[h]) * _silu(gate_ref[:, cols])


def ret_sample(h2, gate, state, prev_out, *, layer, n_heads, dh, rows):
    bsz = h2.shape[0]
    width = n_heads * dh
    depth = state.shape[0]
    assert bsz % rows == 0
    aliased = prev_out is not None

    def seg_spec(sg):
        return pl.BlockSpec((rows, width), lambda i: (i, sg))

    st_spec = pl.BlockSpec((None, rows, n_heads, dh, dh), lambda i: (layer, i, 0, 0, 0))
    in_specs = [seg_spec(SEG_QB), seg_spec(SEG_KB), seg_spec(SEG_VB), seg_spec(0), st_spec]
    args = [h2, h2, h2, gate, state]
    if aliased:
        in_specs.append(pl.BlockSpec(memory_space=pl.ANY))
        args.append(prev_out)
    return pl.pallas_call(
        functools.partial(_ret_sample_kernel, n_heads=n_heads, dh=dh, aliased=aliased),
        grid=(bsz // rows,),
        in_specs=in_specs,
        out_specs=[pl.BlockSpec((rows, width), lambda i: (i, 0)), st_spec],
        out_shape=[jax.ShapeDtypeStruct((bsz, width), F32),
                   jax.ShapeDtypeStruct((depth, bsz, n_heads, dh, dh), F32)],
        input_output_aliases={5: 1} if aliased else {},
        compiler_params=_params(("parallel",)),
        name="ret_sample",
    )(*args)


def _out_proj_kernel(a_ref, r_ref, x_ref, w_ref, g_ref, b_ref, o_ref, *, alpha):
    wa = a_ref.shape[1]
    mixed = (_dot(a_ref[...].astype(BF16), w_ref[0:wa, :])
             + _dot(r_ref[...].astype(BF16), w_ref[wa:, :]))
    o_ref[...] = _layer_norm(alpha * x_ref[...] + mixed, g_ref[...], b_ref[...])


def out_proj(a, r, x, w_out, g, b, *, alpha, tm):
    m, d = x.shape
    assert m % tm == 0
    return pl.pallas_call(
        functools.partial(_out_proj_kernel, alpha=alpha),
        grid=(m // tm,),
        in_specs=[
            pl.BlockSpec((tm, a.shape[1]), lambda i: (i, 0)),
            pl.BlockSpec((tm, r.shape[1]), lambda i: (i, 0)),
            pl.BlockSpec((tm, d), lambda i: (i, 0)),
            pl.BlockSpec(w_out.shape, lambda i: (0, 0)),
            pl.BlockSpec((1, d), lambda i: (0, 0)),
            pl.BlockSpec((1, d), lambda i: (0, 0)),
        ],
        out_specs=pl.BlockSpec((tm, d), lambda i: (i, 0)),
        out_shape=jax.ShapeDtypeStruct((m, d), F32),
        compiler_params=_params(("parallel",)),
        name="out_proj",
    )(a, r, x, w_out, g.reshape(1, d), b.reshape(1, d))


def _row_tile(m, cap):
    t = min(m, cap)
    while m % t:
        t //= 2
    return t


def kernel(x_prompt, x_sample, cache_k, cache_v, state_ret, page_table, w_in, w_out,
           lambda_q1, lambda_k1, lambda_q2, lambda_k2, subln_g,
           ffn1_w1, ffn1_w3, ffn1_w2, ffn2_w1, ffn2_w3, ffn2_w2, ln_g, ln_b):
    bsz_p, s_p, d = x_prompt.shape
    bsz_d, s_d, _ = x_sample.shape
    assert s_d == 1, "one new token per sample sequence"
    depth, _, page, n_ha, dv = cache_k.shape
    n_hb, dh_b = state_ret.shape[2], state_ret.shape[3]
    n_pages = page_table.shape[1]
    past = n_pages * page
    dqk = dv // 2
    a_width = n_ha * dv
    b_width = n_hb * dh_b
    assert w_in.shape[2] == N_SEG * a_width and a_width == b_width
    dff = ffn1_w1.shape[2]
    alpha = (2 * depth) ** 0.25

    m_p = bsz_p * s_p
    m_d = bsz_d * s_d
    xp = x_prompt.reshape(m_p, d)
    xd = x_sample.reshape(m_d, d)

    cos_p, sin_p = rope_table(s_p, dh_b // 2, 0, s_p)
    cos_d, sin_d = rope_table(m_d, dh_b // 2, past, 1)

    w_out_b = w_out.astype(BF16)
    f1 = (ffn1_w1, ffn1_w3, ffn1_w2)
    f2 = (ffn2_w1, ffn2_w3, ffn2_w2)
    pt_flat = page_table.reshape(-1)

    tf = _row_tile(dff, 512)
    tm_ffn_p, tm_ffn_d = _row_tile(m_p, 1024), _row_tile(m_d, 1024)
    tm_out_p, tm_out_d = _row_tile(m_p, 512), _row_tile(m_d, 512)
    tm_proj_p, tm_proj_d = _row_tile(s_p, 512), _row_tile(m_d, 512)
    tq = _row_tile(s_p, 512)
    chunk = _row_tile(s_p, 256)
    ret_rows = 8

    proj_kw = dict(depth=depth, q_scale=dqk ** -0.5, kb_scale=dh_b ** -0.5, dh_b=dh_b)
    ns_p = []
    kv_p = kv_d = ns_d = None
    for l in range(depth):
        lam_init = 0.8 - 0.6 * math.exp(-0.3 * l)
        lam4 = jnp.stack([lambda_q1[l], lambda_k1[l], lambda_q2[l], lambda_k2[l]]).astype(F32)
        ffn_kw = dict(layer=l, alpha=alpha, tf=tf)

        xd, *ffn1_b = ffn_half(xd, *f1, ln_g[l, 0], ln_b[l, 0], tm=tm_ffn_d, **ffn_kw)
        hd, *kv_d, gf, w_in_b = in_proj(xd, w_in, cos_d, sin_d, kv_d, layer=l, tm=tm_proj_d,
                                        h_dtype=F32, **proj_kw)
        attn = attn_sample(hd.reshape(m_d, 6 * n_ha, dv), cache_k, cache_v, pt_flat, lam4, subln_g[l],
                           layer=l, n_pages=n_pages, lam_init=lam_init)
        ret, ns_d = ret_sample(hd, gf, state_ret, ns_d, layer=l, n_heads=n_hb, dh=dh_b, rows=ret_rows)
        xd = out_proj(attn.reshape(m_d, a_width), ret, xd, w_out_b[l], ln_g[l, 1], ln_b[l, 1],
                      alpha=alpha, tm=tm_out_d)
        xd, *ffn2_b = ffn_half(xd, *f2, ln_g[l, 2], ln_b[l, 2], tm=tm_ffn_d, **ffn_kw)

        xp, = ffn_half(xp, *ffn1_b, ln_g[l, 0], ln_b[l, 0], tm=tm_ffn_p, **ffn_kw)
        hp, *kv_p, gf = in_proj(xp, w_in_b, cos_p, sin_p, kv_p, layer=l, tm=tm_proj_p,
                                h_dtype=BF16, **proj_kw)
        attn = attn_prompt(hp, lam4, subln_g[l], bsz=bsz_p, s_len=s_p, n_heads=n_ha, dv=dv,
                           tq=tq, lam_init=lam_init)
        ret, s_fin = ret_prompt(hp.reshape(bsz_p, s_p, -1), gf.reshape(bsz_p, s_p, -1),
                                n_heads=n_hb, dh=dh_b, chunk=chunk)
        xp = out_proj(attn, ret.reshape(m_p, b_width), xp, w_out_b[l], ln_g[l, 1], ln_b[l, 1],
                      alpha=alpha, tm=tm_out_p)
        xp, = ffn_half(xp, *ffn2_b, ln_g[l, 2], ln_b[l, 2], tm=tm_ffn_p, **ffn_kw)
        ns_p.append(s_fin)

    kv_shape_p = (depth, bsz_p, s_p, n_ha, dv)
    kv_shape_d = (depth, bsz_d, s_d, n_ha, dv)
    return (xp.reshape(bsz_p, s_p, d), xd.reshape(bsz_d, s_d, d),
            kv_p[0].reshape(kv_shape_p), kv_p[1].reshape(kv_shape_p), jnp.stack(ns_p),
            kv_d[0].reshape(kv_shape_d), kv_d[1].reshape(kv_shape_d), ns_d)
```

```python
import functools
import math

import numpy as np
import jax
import jax.numpy as jnp
from jax import lax
from jax.experimental import pallas as pl
from jax.experimental.pallas import tpu as pltpu

F32 = jnp.float32
BF16 = jnp.bfloat16

LN_EPS = 1e-5
ROPE_BASE = 10000.0
LANES = 128
VMEM_LIMIT = 56 * 1024 * 1024


def _params(sem, vmem=VMEM_LIMIT):
    return pltpu.CompilerParams(dimension_semantics=sem, vmem_limit_bytes=vmem)


def _layer_norm(y, g, b):
    mu = jnp.mean(y, axis=-1, keepdims=True)
    d = y - mu
    var = jnp.mean(d * d, axis=-1, keepdims=True)
    return d * lax.rsqrt(var + LN_EPS) * g + b


def _rms_norm(y):
    return y * lax.rsqrt(jnp.mean(y * y, axis=-1, keepdims=True) + LN_EPS)


def _silu(a):
    return a * (1.0 / (1.0 + jnp.exp(-a)))


def _dot_nt(a, b):
    return lax.dot_general(a, b, (((1,), (1,)), ((), ())), preferred_element_type=F32)


def _dot_tn(a, b):
    return lax.dot_general(a, b, (((0,), (0,)), ((), ())), preferred_element_type=F32)


def _dot(a, b):
    return jnp.dot(a, b, preferred_element_type=F32)


def _rope_kernel(inv_ref, cos_ref, sin_ref, *, base, period):
    rows = cos_ref.shape[0]
    row = lax.broadcasted_iota(jnp.int32, cos_ref.shape, 0)
    if period == 1:
        pos = jnp.full(cos_ref.shape, base, jnp.int32)
    else:
        assert period == rows
        pos = row + base
    ang = pos.astype(F32) * inv_ref[...]
    cos_ref[...] = jnp.cos(ang)
    sin_ref[...] = jnp.sin(ang)


def rope_table(rows, half, base, period):
    inv = (ROPE_BASE ** (-jnp.arange(half, dtype=F32) / half)).reshape(1, half)
    return pl.pallas_call(
        functools.partial(_rope_kernel, base=base, period=period),
        out_shape=(jax.ShapeDtypeStruct((rows, half), F32),) * 2,
        name="rope_table",
    )(inv)


def _layer_spec(w, layer, block, index_map):
    if w.ndim == 2:
        return pl.BlockSpec(block, index_map)
    return pl.BlockSpec((None,) + block, lambda *g: (layer,) + index_map(*g))


def _ffn_kernel(x_ref, w1_ref, w3_ref, w2_ref, g_ref, b_ref, o_ref, *rest, alpha):
    xb_ref = rest[-1]
    j = pl.program_id(1)

    @pl.when(j == 0)
    def _():
        xb_ref[...] = x_ref[...].astype(BF16)
        o_ref[...] = jnp.zeros_like(o_ref)

    w1, w3, w2 = (r[...].astype(BF16) for r in (w1_ref, w3_ref, w2_ref))
    for dst, w in zip(rest[:-1], (w1, w3, w2)):
        dst[...] = w

    xb = xb_ref[...]
    a = _dot(xb, w1)
    b = _dot(xb, w3)
    h = (_silu(a) * b).astype(BF16)
    o_ref[...] += _dot(h, w2)

    @pl.when(j == pl.num_programs(1) - 1)
    def _():
        y = alpha * x_ref[...] + 0.5 * o_ref[...]
        o_ref[...] = _layer_norm(y, g_ref[...], b_ref[...])


def ffn_half(x, w1, w3, w2, g, b, *, layer, alpha, tm, tf):
    m, d = x.shape
    dff = w1.shape[-1]
    assert m % tm == 0 and dff % tf == 0
    up = ((d, tf), lambda i, j: (0, j))
    down = ((tf, d), lambda i, j: (j, 0))
    out_specs = [pl.BlockSpec((tm, d), lambda i, j: (i, 0))]
    out_shape = [jax.ShapeDtypeStruct((m, d), F32)]
    if w1.dtype != BF16:
        assert m == tm, "bf16 weight tiles are emitted once, by a single row tile"
        out_specs += [pl.BlockSpec(*up), pl.BlockSpec(*up), pl.BlockSpec(*down)]
        out_shape += [jax.ShapeDtypeStruct(w.shape[-2:], BF16) for w in (w1, w3, w2)]
    return pl.pallas_call(
        functools.partial(_ffn_kernel, alpha=alpha),
        grid=(m // tm, dff // tf),
        in_specs=[
            pl.BlockSpec((tm, d), lambda i, j: (i, 0)),
            _layer_spec(w1, layer, *up),
            _layer_spec(w3, layer, *up),
            _layer_spec(w2, layer, *down),
            pl.BlockSpec((1, d), lambda i, j: (0, 0)),
            pl.BlockSpec((1, d), lambda i, j: (0, 0)),
        ],
        out_specs=out_specs,
        out_shape=out_shape,
        scratch_shapes=[pltpu.VMEM((tm, d), BF16)],
        compiler_params=_params(("parallel", "arbitrary")),
        name="ffn_half",
    )(x, w1, w3, w2, g.reshape(1, d), b.reshape(1, d))


N_SEG = 7
SEG_KA, SEG_VA, SEG_QB, SEG_KB, SEG_VB, SEG_G = 1, 2, 3, 4, 5, 6


def _proj_kernel(x_ref, w_ref, cos_ref, sin_ref, *rest, n_alias, q_scale, kb_scale, dh_b):
    h_ref, kf_ref, vf_ref, gf_ref = rest[n_alias:n_alias + 4]
    xb_ref = rest[-1]
    j = pl.program_id(1)

    @pl.when(j == 0)
    def _():
        xb_ref[...] = x_ref[...].astype(BF16)

    w = w_ref[...].astype(BF16)
    for dst in rest[n_alias + 4:-1]:
        dst[...] = w
    acc = _dot(xb_ref[...], w)
    hd = h_ref.dtype

    @pl.when(j == 0)
    def _():
        h_ref[...] = (acc * q_scale).astype(hd)

    @pl.when(j == SEG_KA)
    def _():
        kf_ref[...] = acc
        h_ref[...] = acc.astype(hd)

    @pl.when(j == SEG_VA)
    def _():
        vf_ref[...] = acc
        h_ref[...] = acc.astype(hd)

    def rotary(scale):
        cos = cos_ref[...]
        sin = sin_ref[...]
        half = dh_b // 2
        for h in range(acc.shape[1] // dh_b):
            x1 = acc[:, h * dh_b:h * dh_b + half]
            x2 = acc[:, h * dh_b + half:(h + 1) * dh_b]
            h_ref[:, h * dh_b:h * dh_b + half] = ((x1 * cos - x2 * sin) * scale).astype(hd)
            h_ref[:, h * dh_b + half:(h + 1) * dh_b] = ((x2 * cos + x1 * sin) * scale).astype(hd)

    @pl.when(j == SEG_QB)
    def _():
        rotary(1.0)

    @pl.when(j == SEG_KB)
    def _():
        rotary(kb_scale)

    @pl.when(j == SEG_VB)
    def _():
        h_ref[...] = acc.astype(hd)

    @pl.when(j == SEG_G)
    def _():
        gf_ref[...] = acc


def in_proj(x, w_in, cos, sin, prev_kv, *, layer, depth, tm, h_dtype, q_scale, kb_scale, dh_b):
    m, d = x.shape
    seg = w_in.shape[-1] // N_SEG
    n_tab = cos.shape[0] // tm
    assert m % tm == 0 and cos.shape[0] % tm == 0
    kv_out = jax.ShapeDtypeStruct((depth, m, seg), F32)
    kv_spec = pl.BlockSpec((None, tm, seg), lambda i, j: (layer, i, 0))
    w_block = ((d, seg), lambda i, j: (0, j))
    in_specs = [
        pl.BlockSpec((tm, d), lambda i, j: (i, 0)),
        _layer_spec(w_in, layer, *w_block),
        pl.BlockSpec((tm, cos.shape[1]), lambda i, j: (i % n_tab, 0)),
        pl.BlockSpec((tm, cos.shape[1]), lambda i, j: (i % n_tab, 0)),
    ]
    args = [x, w_in, cos, sin]
    aliases = {}
    if prev_kv is not None:
        in_specs += [pl.BlockSpec(memory_space=pl.ANY)] * 2
        args += list(prev_kv)
        aliases = {4: 1, 5: 2}
    out_specs = [
        pl.BlockSpec((tm, seg), lambda i, j: (i, jnp.minimum(j, SEG_VB))),
        kv_spec,
        kv_spec,
        pl.BlockSpec((tm, seg), lambda i, j: (i, 0)),
    ]
    out_shape = [jax.ShapeDtypeStruct((m, SEG_G * seg), h_dtype), kv_out, kv_out,
                 jax.ShapeDtypeStruct((m, seg), F32)]
    if w_in.dtype != BF16:
        assert m == tm, "bf16 weight tiles are emitted once, by a single row tile"
        out_specs.append(pl.BlockSpec(*w_block))
        out_shape.append(jax.ShapeDtypeStruct(w_in.shape[-2:], BF16))
    return pl.pallas_call(
        functools.partial(_proj_kernel, n_alias=len(aliases), q_scale=q_scale, kb_scale=kb_scale,
                          dh_b=dh_b),
        grid=(m // tm, N_SEG),
        in_specs=in_specs,
        out_specs=out_specs,
        out_shape=out_shape,
        input_output_aliases=aliases,
        scratch_shapes=[pltpu.VMEM((tm, d), BF16)],
        compiler_params=_params(("parallel", "arbitrary")),
        name="in_proj",
    )(*args)


PAST_SPAN = 2


def _diff_lambda(lam4_ref, lam_init):
    v = lam4_ref[...]
    t1 = jnp.sum(v[0:1] * v[1:2], axis=-1, keepdims=True)
    t2 = jnp.sum(v[2:3] * v[3:4], axis=-1, keepdims=True)
    return jnp.exp(t1) - jnp.exp(t2) + lam_init


def _prompt_attention(lam, g_ref, q_ref, k_ref, v_ref, o_ref,
                      vt_ref, m1_ref, l1_ref, a1_ref, m2_ref, l2_ref, a2_ref, *, tq, lam_init):
    qi = pl.program_id(2)
    tk = vt_ref.shape[2]
    per_tile = tq // tk

    @pl.when(qi == 0)
    def _():
        for j in range(vt_ref.shape[0]):
            vt_ref[j] = v_ref[j * tk:(j + 1) * tk, :].astype(F32).T.astype(BF16)

    q = q_ref[...]
    lane = lax.broadcasted_iota(jnp.int32, q.shape, 1)
    half = q.shape[1] // 2
    zero = jnp.zeros_like(q)
    q1 = jnp.where(lane < half, q, zero)
    q2 = jnp.where(lane >= half, q, zero)

    for m_ref, l_ref, a_ref in ((m1_ref, l1_ref, a1_ref), (m2_ref, l2_ref, a2_ref)):
        m_ref[...] = jnp.full(m_ref.shape, -jnp.inf, F32)
        l_ref[...] = jnp.zeros(l_ref.shape, F32)
        a_ref[...] = jnp.zeros(a_ref.shape, F32)

    def update(s, vt, m_ref, l_ref, a_ref):
        m_old = m_ref[...]
        m_new = jnp.maximum(m_old, jnp.max(s, axis=0, keepdims=True))
        alpha = jnp.exp2(m_old - m_new)
        p = jnp.exp2(s - m_new)
        l_ref[...] = alpha * l_ref[...] + jnp.sum(p, axis=0, keepdims=True)
        a_ref[...] = alpha * a_ref[...] + _dot(vt, p.astype(BF16))
        m_ref[...] = m_new

    def block(ki, span, diag):
        start = pl.multiple_of(ki * tk, tk)
        k = k_ref[pl.ds(start, span * tk), :]
        vt = jnp.concatenate([vt_ref[ki + u] for u in range(span)], axis=1)
        s1 = _dot_nt(k, q1)
        s2 = _dot_nt(k, q2)
        if diag is not None:
            r = lax.broadcasted_iota(jnp.int32, s1.shape, 0) + diag * tk
            c = lax.broadcasted_iota(jnp.int32, s1.shape, 1)
            s1 = jnp.where(r <= c, s1, -jnp.inf)
            s2 = jnp.where(r <= c, s2, -jnp.inf)
        update(s1, vt, m1_ref, l1_ref, a1_ref)
        update(s2, vt, m2_ref, l2_ref, a2_ref)

    past = qi * per_tile

    def body(kk, carry):
        block(kk * PAST_SPAN, PAST_SPAN, None)
        return carry

    lax.fori_loop(0, past // PAST_SPAN, body, 0)
    for u in range(PAST_SPAN - 1):
        @pl.when(past % PAST_SPAN > u)
        def _():
            block(past - past % PAST_SPAN + u, 1, None)
    for j in range(per_tile):
        block(past + j, 1, j)

    ot = a1_ref[...] / l1_ref[...] - lam * (a2_ref[...] / l2_ref[...])
    ot = ot * lax.rsqrt(jnp.mean(ot * ot, axis=0, keepdims=True) + LN_EPS)
    o_ref[...] = (ot.T * g_ref[...] * (1.0 - lam_init)).astype(o_ref.dtype)


def _sample_attention(lam, g_ref, q_ref, kn_ref, vn_ref, k_refs, v_refs, o_ref, *, lam_init):
    nh, dv = q_ref.shape[1], q_ref.shape[2]
    half = dv // 2

    q8 = q_ref[0].astype(F32)
    q16 = jnp.concatenate([q8, q8], axis=0)
    row = lax.broadcasted_iota(jnp.int32, q16.shape, 0)
    lane = lax.broadcasted_iota(jnp.int32, q16.shape, 1)
    qm = jnp.where(lane // half == row // nh, q16, 0.0).astype(BF16)

    kn = kn_ref[0].astype(BF16).astype(F32)
    vn = vn_ref[0].astype(BF16).astype(F32)
    s_self = jnp.sum(qm.astype(F32) * jnp.concatenate([kn, kn], axis=0), axis=-1, keepdims=True)

    n_col = k_refs[0].shape[0] * nh
    srow = lax.broadcasted_iota(jnp.int32, (2 * nh, n_col), 0)
    scol = lax.broadcasted_iota(jnp.int32, (2 * nh, n_col), 1)
    valid = (scol & (nh - 1)) == (srow & (nh - 1))

    scores = []
    for k_ref in k_refs:
        kf = k_ref[...].reshape(n_col, dv).astype(BF16)
        scores.append(jnp.where(valid, _dot_nt(qm, kf), -jnp.inf))
    mx = functools.reduce(jnp.maximum, scores)
    m = jnp.maximum(s_self, jnp.max(mx, axis=-1, keepdims=True))
    w_self = jnp.exp2(s_self - m)
    probs = [jnp.exp2(s - m) for s in scores]
    l = w_self + jnp.sum(functools.reduce(jnp.add, probs), axis=-1, keepdims=True)
    acc = w_self * jnp.concatenate([vn, vn], axis=0)
    for p, v_ref in zip(probs, v_refs):
        acc = acc + _dot(p.astype(BF16), v_ref[...].reshape(n_col, dv).astype(BF16))

    acc = acc / l
    o = acc[0:nh] - lam * acc[nh:2 * nh]
    o_ref[0] = _rms_norm(o) * g_ref[...] * (1.0 - lam_init)


def _attn_kernel(pt_ref, lam4_ref, g_ref, q_ref, k_ref, v_ref, qd_ref, knd_ref, vnd_ref, *rest,
                 n_pages, tq, lam_init):
    kd_refs = rest[:n_pages]
    vd_refs = rest[n_pages:2 * n_pages]
    o_ref, od_ref = rest[2 * n_pages:2 * n_pages + 2]
    scratch = rest[2 * n_pages + 2:]
    lam = _diff_lambda(lam4_ref, lam_init)
    _sample_attention(lam, g_ref, qd_ref, knd_ref, vnd_ref, kd_refs, vd_refs, od_ref,
                      lam_init=lam_init)
    _prompt_attention(lam, g_ref, q_ref, k_ref, v_ref, o_ref, *scratch, tq=tq, lam_init=lam_init)


def diff_attention(h, hd3, cache_k, cache_v, pt_flat, lam4, subln_g, *,
                   layer, bsz, s_len, n_pages, tq, tk, lam_init):
    _, _, page, nh, dv = cache_k.shape
    nq = s_len // tq
    assert s_len % tq == 0 and tq % tk == 0 and nh & (nh - 1) == 0
    assert hd3.shape[0] == bsz * nh * nq, "one sample sequence per prompt (batch, head, q tile)"

    def seq(b, hh, qi):
        return (b * nh + hh) * nq + qi

    def page_spec(r):
        return pl.BlockSpec(
            (None, None, page, nh, dv),
            lambda b, hh, qi, pt: (layer, pt[seq(b, hh, qi) * n_pages + r], 0, 0, 0))

    def new_token_spec(sg):
        return pl.BlockSpec((1, nh, dv), lambda b, hh, qi, pt: (seq(b, hh, qi), sg, 0))

    stat = pltpu.VMEM((1, tq), F32)
    accs = pltpu.VMEM((dv, tq), F32)
    grid_spec = pltpu.PrefetchScalarGridSpec(
        num_scalar_prefetch=1,
        grid=(bsz, nh, nq),
        in_specs=[
            pl.BlockSpec(lam4.shape, lambda b, hh, qi, pt: (0, 0)),
            pl.BlockSpec((1, dv), lambda b, hh, qi, pt: (0, 0)),
            pl.BlockSpec((tq, dv), lambda b, hh, qi, pt: (b * nq + qi, hh)),
            pl.BlockSpec((s_len, dv), lambda b, hh, qi, pt: (b, nh + hh)),
            pl.BlockSpec((s_len, dv), lambda b, hh, qi, pt: (b, 2 * nh + hh)),
            new_token_spec(0), new_token_spec(SEG_KA), new_token_spec(SEG_VA),
        ] + [page_spec(r) for r in range(n_pages)] * 2,
        out_specs=[
            pl.BlockSpec((tq, dv), lambda b, hh, qi, pt: (b * nq + qi, hh)),
            pl.BlockSpec((1, nh, dv), lambda b, hh, qi, pt: (seq(b, hh, qi), 0, 0)),
        ],
        scratch_shapes=[pltpu.VMEM((s_len // tk, dv, tk), BF16),
                        stat, stat, accs, stat, stat, accs],
    )
    return pl.pallas_call(
        functools.partial(_attn_kernel, n_pages=n_pages, tq=tq, lam_init=lam_init),
        grid_spec=grid_spec,
        out_shape=[jax.ShapeDtypeStruct((bsz * s_len, nh * dv), BF16),
                   jax.ShapeDtypeStruct((hd3.shape[0], nh, dv), F32)],
        compiler_params=_params(("parallel", "parallel", "arbitrary")),
        name="diff_attention",
    )(pt_flat, lam4, subln_g.reshape(1, dv), h, h, h, hd3, hd3, hd3,
      *([cache_k] * n_pages), *([cache_v] * n_pages))


def _log_gamma(h):
    return float(np.log(np.float32(1.0) - np.float32(2.0) ** np.float32(-5.0 - h)))


def _ret_prompt_kernel(q_ref, k_ref, v_ref, gate_ref, r_ref, sfin_ref,
                       st_ref, dm_ref, qd_ref, kd_ref, *, n_heads, dh):
    c = pl.program_id(0)
    bsz, ch = q_ref.shape[0], q_ref.shape[1]

    @pl.when(c == 0)
    def _():
        st_ref[...] = jnp.zeros_like(st_ref)
        i = lax.broadcasted_iota(jnp.int32, (ch, ch), 0)
        jj = lax.broadcasted_iota(jnp.int32, (ch, ch), 1)
        rel = (i - jj).astype(F32)
        idx = lax.broadcasted_iota(jnp.int32, (ch, 1), 0).astype(F32)
        for h in range(n_heads):
            lg = _log_gamma(h)
            dm_ref[h] = jnp.where(rel >= 0, jnp.exp(lg * jnp.maximum(rel, 0.0)), 0.0)
            qd_ref[h] = jnp.exp(lg * (idx + 1.0))
            kd_ref[h] = jnp.exp(lg * (ch - 1.0 - idx))

    for b in range(bsz):
        for h in range(n_heads):
            cols = slice(h * dh, (h + 1) * dh)
            q = q_ref[b, :, cols]
            k = k_ref[b, :, cols]
            v = v_ref[b, :, cols]
            s0 = st_ref[b * n_heads + h]
            inner = (_dot_nt(q, k) * dm_ref[h]).astype(BF16)
            o = _dot(inner, v) + qd_ref[h] * _dot(q, s0.astype(BF16))
            kk = (k.astype(F32) * kd_ref[h]).astype(BF16)
            chunk_decay = float(np.exp(np.float32(_log_gamma(h)) * np.float32(ch)))
            st_ref[b * n_heads + h] = s0 * chunk_decay + _dot_tn(kk, v)
            r_ref[b, :, cols] = (_rms_norm(o) * _silu(gate_ref[b, :, cols])).astype(r_ref.dtype)

    @pl.when(c == pl.num_programs(0) - 1)
    def _():
        for b in range(bsz):
            for h in range(n_heads):
                sfin_ref[b, h] = st_ref[b * n_heads + h]


def ret_prompt(h3, gate3, *, n_heads, dh, chunk):
    bsz, s_len, _ = h3.shape
    width = n_heads * dh
    assert s_len % chunk == 0

    def seg_spec(sg):
        return pl.BlockSpec((bsz, chunk, width), lambda c: (0, c, sg))

    return pl.pallas_call(
        functools.partial(_ret_prompt_kernel, n_heads=n_heads, dh=dh),
        grid=(s_len // chunk,),
        in_specs=[seg_spec(SEG_QB), seg_spec(SEG_KB), seg_spec(SEG_VB), seg_spec(0)],
        out_specs=[
            pl.BlockSpec((bsz, chunk, width), lambda c: (0, c, 0)),
            pl.BlockSpec((bsz, n_heads, dh, dh), lambda c: (0, 0, 0, 0)),
        ],
        out_shape=[jax.ShapeDtypeStruct((bsz, s_len, width), BF16),
                   jax.ShapeDtypeStruct((bsz, n_heads, dh, dh), F32)],
        scratch_shapes=[
            pltpu.VMEM((bsz * n_heads, dh, dh), F32),
            pltpu.VMEM((n_heads, chunk, chunk), F32),
            pltpu.VMEM((n_heads, chunk, 1), F32),
            pltpu.VMEM((n_heads, chunk, 1), F32),
        ],
        compiler_params=_params(("arbitrary",)),
        name="ret_prompt",
    )(h3, h3, h3, gate3)


def _ret_sample_kernel(q_ref, k_ref, v_ref, gate_ref, st_ref, *rest, n_heads, dh, aliased):
    if aliased:
        rest = rest[1:]
    r_ref, snew_ref = rest
    rows = q_ref.shape[0]
    row = lax.broadcasted_iota(jnp.int32, (rows, dh), 0)

    def body(bi, o_acc):
        out = []
        for h in range(n_heads):
            cols = slice(h * dh, (h + 1) * dh)
            sel = row == bi
            qb = jnp.where(sel, q_ref[:, cols], 0.0).astype(BF16)
            kb = jnp.where(sel, k_ref[:, cols], 0.0).astype(BF16)
            vb = v_ref[:, cols].astype(BF16)
            gamma = float(np.exp(np.float32(_log_gamma(h))))
            s_new = st_ref[bi, h] * gamma + _dot_tn(kb, vb)
            snew_ref[bi, h] = s_new
            out.append(o_acc[h] + _dot(qb, s_new.astype(BF16)))
        return tuple(out)

    o = lax.fori_loop(0, rows, body, tuple(jnp.zeros((rows, dh), F32) for _ in range(n_heads)))
    for h in range(n_heads):
        cols = slice(h * dh, (h + 1) * dh)
        r_ref[:, cols] = _rms_norm(o[h]) * _silu(gate_ref[:, cols])


def ret_sample(h2, gate, state, prev_out, *, layer, n_heads, dh, rows):
    bsz = h2.shape[0]
    width = n_heads * dh
    depth = state.shape[0]
    assert bsz % rows == 0
    aliased = prev_out is not None

    def seg_spec(sg):
        return pl.BlockSpec((rows, width), lambda i: (i, sg))

    st_spec = pl.BlockSpec((None, rows, n_heads, dh, dh), lambda i: (layer, i, 0, 0, 0))
    in_specs = [seg_spec(SEG_QB), seg_spec(SEG_KB), seg_spec(SEG_VB), seg_spec(0), st_spec]
    args = [h2, h2, h2, gate, state]
    if aliased:
        in_specs.append(pl.BlockSpec(memory_space=pl.ANY))
        args.append(prev_out)
    return pl.pallas_call(
        functools.partial(_ret_sample_kernel, n_heads=n_heads, dh=dh, aliased=aliased),
        grid=(bsz // rows,),
        in_specs=in_specs,
        out_specs=[pl.BlockSpec((rows, width), lambda i: (i, 0)), st_spec],
        out_shape=[jax.ShapeDtypeStruct((bsz, width), F32),
                   jax.ShapeDtypeStruct((depth, bsz, n_heads, dh, dh), F32)],
        input_output_aliases={5: 1} if aliased else {},
        compiler_params=_params(("parallel",)),
        name="ret_sample",
    )(*args)


def _out_proj_kernel(a_ref, r_ref, x_ref, w_ref, g_ref, b_ref, o_ref, *, alpha):
    wa = a_ref.shape[1]
    mixed = (_dot(a_ref[...].astype(BF16), w_ref[0:wa, :])
             + _dot(r_ref[...].astype(BF16), w_ref[wa:, :]))
    o_ref[...] = _layer_norm(alpha * x_ref[...] + mixed, g_ref[...], b_ref[...])


def out_proj(a, r, x, w_out, g, b, *, alpha, tm):
    m, d = x.shape
    assert m % tm == 0
    return pl.pallas_call(
        functools.partial(_out_proj_kernel, alpha=alpha),
        grid=(m // tm,),
        in_specs=[
            pl.BlockSpec((tm, a.shape[1]), lambda i: (i, 0)),
            pl.BlockSpec((tm, r.shape[1]), lambda i: (i, 0)),
            pl.BlockSpec((tm, d), lambda i: (i, 0)),
            pl.BlockSpec(w_out.shape, lambda i: (0, 0)),
            pl.BlockSpec((1, d), lambda i: (0, 0)),
            pl.BlockSpec((1, d), lambda i: (0, 0)),
        ],
        out_specs=pl.BlockSpec((tm, d), lambda i: (i, 0)),
        out_shape=jax.ShapeDtypeStruct((m, d), F32),
        compiler_params=_params(("parallel",)),
        name="out_proj",
    )(a, r, x, w_out, g.reshape(1, d), b.reshape(1, d))


ROW_TILE = 512
FF_TILE = 512
RET_CHUNK = 256
F32_SUBLANES = 8


def _row_tile(m, cap):
    t = min(m, cap)
    while m % t:
        t //= 2
    return t


def kernel(x_prompt, x_sample, cache_k, cache_v, state_ret, page_table, w_in, w_out,
           lambda_q1, lambda_k1, lambda_q2, lambda_k2, subln_g,
           ffn1_w1, ffn1_w3, ffn1_w2, ffn2_w1, ffn2_w3, ffn2_w2, ln_g, ln_b):
    bsz_p, s_p, d = x_prompt.shape
    bsz_d, s_d, _ = x_sample.shape
    assert s_d == 1, "one new token per sample sequence"
    depth, _, page, n_ha, dv = cache_k.shape
    n_hb, dh_b = state_ret.shape[2], state_ret.shape[3]
    n_pages = page_table.shape[1]
    past = n_pages * page
    dqk = dv // 2
    a_width = n_ha * dv
    b_width = n_hb * dh_b
    assert w_in.shape[2] == N_SEG * a_width and a_width == b_width
    dff = ffn1_w1.shape[2]
    alpha = (2 * depth) ** 0.25

    m_p = bsz_p * s_p
    m_d = bsz_d * s_d
    xp = x_prompt.reshape(m_p, d)
    xd = x_sample.reshape(m_d, d)

    cos_p, sin_p = rope_table(s_p, dh_b // 2, 0, s_p)
    cos_d, sin_d = rope_table(m_d, dh_b // 2, past, 1)

    w_out_b = w_out.astype(BF16)
    f1 = (ffn1_w1, ffn1_w3, ffn1_w2)
    f2 = (ffn2_w1, ffn2_w3, ffn2_w2)
    pt_flat = page_table.reshape(-1)

    tf = _row_tile(dff, FF_TILE)
    tm_ffn_p, tm_ffn_d = _row_tile(m_p, ROW_TILE), _row_tile(m_d, ROW_TILE)
    tm_out_p, tm_out_d = tm_ffn_p, tm_ffn_d
    tm_proj_p, tm_proj_d = _row_tile(s_p, ROW_TILE), _row_tile(m_d, ROW_TILE)
    tq = _row_tile(s_p, ROW_TILE)
    tk = tq
    chunk = _row_tile(s_p, RET_CHUNK)
    ret_rows = F32_SUBLANES

    proj_kw = dict(depth=depth, q_scale=dqk ** -0.5 * math.log2(math.e), kb_scale=dh_b ** -0.5,
                   dh_b=dh_b)
    ns_p = []
    kv_p = kv_d = ns_d = None
    for l in range(depth):
        lam_init = 0.8 - 0.6 * math.exp(-0.3 * l)
        lam4 = jnp.stack([lambda_q1[l], lambda_k1[l], lambda_q2[l], lambda_k2[l]]).astype(F32)
        ffn_kw = dict(layer=l, alpha=alpha, tf=tf)

        xd, *ffn1_b = ffn_half(xd, *f1, ln_g[l, 0], ln_b[l, 0], tm=tm_ffn_d, **ffn_kw)
        hd, *kv_d, gf, w_in_b = in_proj(xd, w_in, cos_d, sin_d, kv_d, layer=l, tm=tm_proj_d,
                                        h_dtype=F32, **proj_kw)
        xp, = ffn_half(xp, *ffn1_b, ln_g[l, 0], ln_b[l, 0], tm=tm_ffn_p, **ffn_kw)
        hp, *kv_p, gf_p = in_proj(xp, w_in_b, cos_p, sin_p, kv_p, layer=l, tm=tm_proj_p,
                                  h_dtype=BF16, **proj_kw)

        attn_p, attn_d = diff_attention(hp, hd.reshape(m_d, 6 * n_ha, dv), cache_k, cache_v, pt_flat,
                                        lam4, subln_g[l], layer=l, bsz=bsz_p, s_len=s_p,
                                        n_pages=n_pages, tq=tq, tk=tk, lam_init=lam_init)

        ret, ns_d = ret_sample(hd, gf, state_ret, ns_d, layer=l, n_heads=n_hb, dh=dh_b, rows=ret_rows)
        xd = out_proj(attn_d.reshape(m_d, a_width), ret, xd, w_out_b[l], ln_g[l, 1], ln_b[l, 1],
                      alpha=alpha, tm=tm_out_d)
        xd, *ffn2_b = ffn_half(xd, *f2, ln_g[l, 2], ln_b[l, 2], tm=tm_ffn_d, **ffn_kw)

        ret, s_fin = ret_prompt(hp.reshape(bsz_p, s_p, -1), gf_p.reshape(bsz_p, s_p, -1),
                                n_heads=n_hb, dh=dh_b, chunk=chunk)
        xp = out_proj(attn_p, ret.reshape(m_p, b_width), xp, w_out_b[l], ln_g[l, 1], ln_b[l, 1],
                      alpha=alpha, tm=tm_out_p)
        xp, = ffn_half(xp, *ffn2_b, ln_g[l, 2], ln_b[l, 2], tm=tm_ffn_p, **ffn_kw)
        ns_p.append(s_fin)

    kv_shape_p = (depth, bsz_p, s_p, n_ha, dv)
    kv_shape_d = (depth, bsz_d, s_d, n_ha, dv)
    return (xp.reshape(bsz_p, s_p, d), xd.reshape(bsz_d, s_d, d),
            kv_p[0].reshape(kv_shape_p), kv_p[1].reshape(kv_shape_p), jnp.stack(ns_p),
            kv_d[0].reshape(kv_shape_d), kv_d[1].reshape(kv_shape_d), ns_d)
```

```python
import functools
import math

import numpy as np
import jax
import jax.numpy as jnp
from jax import lax
from jax.experimental import pallas as pl
from jax.experimental.pallas import tpu as pltpu

F32 = jnp.float32
BF16 = jnp.bfloat16

LN_EPS = 1e-5
ROPE_BASE = 10000.0
LANES = 128
VMEM_LIMIT = 56 * 1024 * 1024


def _params(sem, vmem=VMEM_LIMIT):
    return pltpu.CompilerParams(dimension_semantics=sem, vmem_limit_bytes=vmem)


def _layer_norm(y, g, b):
    mu = jnp.mean(y, axis=-1, keepdims=True)
    d = y - mu
    var = jnp.mean(d * d, axis=-1, keepdims=True)
    return d * lax.rsqrt(var + LN_EPS) * g + b


def _rms_norm(y):
    return y * lax.rsqrt(jnp.mean(y * y, axis=-1, keepdims=True) + LN_EPS)


def _silu(a):
    return a * (1.0 / (1.0 + jnp.exp(-a)))


def _dot_nt(a, b):
    return lax.dot_general(a, b, (((1,), (1,)), ((), ())), preferred_element_type=F32)


def _dot_tn(a, b):
    return lax.dot_general(a, b, (((0,), (0,)), ((), ())), preferred_element_type=F32)


def _dot(a, b):
    return jnp.dot(a, b, preferred_element_type=F32)


def _rope_kernel(inv_ref, cos_ref, sin_ref, *, base, period):
    rows = cos_ref.shape[0]
    row = lax.broadcasted_iota(jnp.int32, cos_ref.shape, 0)
    if period == 1:
        pos = jnp.full(cos_ref.shape, base, jnp.int32)
    else:
        assert period == rows
        pos = row + base
    ang = pos.astype(F32) * inv_ref[...]
    cos_ref[...] = jnp.cos(ang)
    sin_ref[...] = jnp.sin(ang)


def rope_table(rows, half, base, period):
    inv = (ROPE_BASE ** (-jnp.arange(half, dtype=F32) / half)).reshape(1, half)
    return pl.pallas_call(
        functools.partial(_rope_kernel, base=base, period=period),
        out_shape=(jax.ShapeDtypeStruct((rows, half), F32),) * 2,
        name="rope_table",
    )(inv)


def _layer_spec(w, layer, block, index_map):
    if w.ndim == 2:
        return pl.BlockSpec(block, index_map)
    return pl.BlockSpec((None,) + block, lambda *g: (layer,) + index_map(*g))


def _ffn_kernel(x_ref, w1_ref, w3_ref, w2_ref, g_ref, b_ref, *rest, alpha, n_cast):
    cast_in, o_ref, outs, xb_ref = rest[:n_cast], rest[n_cast], rest[n_cast + 1:-1], rest[-1]
    emit_out, cast_out = outs[:len(outs) - n_cast], outs[len(outs) - n_cast:]
    j = pl.program_id(1)

    @pl.when(j == 0)
    def _():
        xb_ref[...] = x_ref[...].astype(BF16)
        o_ref[...] = jnp.zeros_like(o_ref)

    w1, w3, w2 = (r[...].astype(BF16) for r in (w1_ref, w3_ref, w2_ref))
    for dst, w in zip(emit_out, (w1, w3, w2)):
        dst[...] = w
    for src, dst in zip(cast_in, cast_out):
        dst[...] = src[...].astype(BF16)

    xb = xb_ref[...]
    a = _dot(xb, w1)
    b = _dot(xb, w3)
    h = (_silu(a) * b).astype(BF16)
    o_ref[...] += _dot(h, w2)

    @pl.when(j == pl.num_programs(1) - 1)
    def _():
        y = alpha * x_ref[...] + 0.5 * o_ref[...]
        o_ref[...] = _layer_norm(y, g_ref[...], b_ref[...])


def ffn_half(x, w1, w3, w2, g, b, *, layer, alpha, tm, tf, cast=None, cast_layer=None):
    m, d = x.shape
    dff = w1.shape[-1]
    assert m % tm == 0 and dff % tf == 0
    n_i = m // tm
    up = ((d, tf), lambda i, j: (0, j))
    down = ((tf, d), lambda i, j: (j, 0))
    in_specs = [
        pl.BlockSpec((tm, d), lambda i, j: (i, 0)),
        _layer_spec(w1, layer, *up),
        _layer_spec(w3, layer, *up),
        _layer_spec(w2, layer, *down),
        pl.BlockSpec((1, d), lambda i, j: (0, 0)),
        pl.BlockSpec((1, d), lambda i, j: (0, 0)),
    ]
    args = [x, w1, w3, w2, g.reshape(1, d), b.reshape(1, d)]
    out_specs = [pl.BlockSpec((tm, d), lambda i, j: (i, 0))]
    out_shape = [jax.ShapeDtypeStruct((m, d), F32)]
    if w1.dtype != BF16:
        assert n_i == 1, "bf16 weight tiles are emitted once, by a single row tile"
        out_specs += [pl.BlockSpec(*up), pl.BlockSpec(*up), pl.BlockSpec(*down)]
        out_shape += [jax.ShapeDtypeStruct(w.shape[-2:], BF16) for w in (w1, w3, w2)]
    if cast is not None:
        ds = d // n_i
        assert d % n_i == 0 and ds % LANES == 0
        up_c = ((ds, tf), lambda i, j: (i, j))
        down_c = ((tf, ds), lambda i, j: (j, i))
        for w, blk in zip(cast, (up_c, up_c, down_c)):
            in_specs.append(_layer_spec(w, cast_layer, *blk))
            out_specs.append(pl.BlockSpec(*blk))
            out_shape.append(jax.ShapeDtypeStruct(w.shape[-2:], BF16))
        args += list(cast)
    return pl.pallas_call(
        functools.partial(_ffn_kernel, alpha=alpha, n_cast=0 if cast is None else len(cast)),
        grid=(n_i, dff // tf),
        in_specs=in_specs,
        out_specs=out_specs,
        out_shape=out_shape,
        scratch_shapes=[pltpu.VMEM((tm, d), BF16)],
        compiler_params=_params(("parallel", "arbitrary")),
        name="ffn_half",
    )(*args)


N_SEG = 7
SEG_KA, SEG_VA, SEG_QB, SEG_KB, SEG_VB, SEG_G = 1, 2, 3, 4, 5, 6


def _proj_kernel(x_ref, w_ref, cos_ref, sin_ref, *rest, n_alias, q_scale, kb_scale, dh_b):
    h_ref, kf_ref, vf_ref, gf_ref = rest[n_alias:n_alias + 4]
    xb_ref = rest[-1]
    j = pl.program_id(1)

    @pl.when(j == 0)
    def _():
        xb_ref[...] = x_ref[...].astype(BF16)

    w = w_ref[...].astype(BF16)
    for dst in rest[n_alias + 4:-1]:
        dst[...] = w
    acc = _dot(xb_ref[...], w)
    hd = h_ref.dtype

    @pl.when(j == 0)
    def _():
        h_ref[...] = (acc * q_scale).astype(hd)

    @pl.when(j == SEG_KA)
    def _():
        kf_ref[...] = acc
        h_ref[...] = acc.astype(hd)

    @pl.when(j == SEG_VA)
    def _():
        vf_ref[...] = acc
        h_ref[...] = acc.astype(hd)

    def rotary(scale):
        cos = cos_ref[...]
        sin = sin_ref[...]
        half = dh_b // 2
        for h in range(acc.shape[1] // dh_b):
            x1 = acc[:, h * dh_b:h * dh_b + half]
            x2 = acc[:, h * dh_b + half:(h + 1) * dh_b]
            h_ref[:, h * dh_b:h * dh_b + half] = ((x1 * cos - x2 * sin) * scale).astype(hd)
            h_ref[:, h * dh_b + half:(h + 1) * dh_b] = ((x2 * cos + x1 * sin) * scale).astype(hd)

    @pl.when(j == SEG_QB)
    def _():
        rotary(1.0)

    @pl.when(j == SEG_KB)
    def _():
        rotary(kb_scale)

    @pl.when(j == SEG_VB)
    def _():
        h_ref[...] = acc.astype(hd)

    @pl.when(j == SEG_G)
    def _():
        gf_ref[...] = acc


def in_proj(x, w_in, cos, sin, prev_kv, *, layer, depth, tm, h_dtype, q_scale, kb_scale, dh_b):
    m, d = x.shape
    seg = w_in.shape[-1] // N_SEG
    n_tab = cos.shape[0] // tm
    assert m % tm == 0 and cos.shape[0] % tm == 0
    kv_out = jax.ShapeDtypeStruct((depth, m, seg), F32)
    kv_spec = pl.BlockSpec((None, tm, seg), lambda i, j: (layer, i, 0))
    w_block = ((d, seg), lambda i, j: (0, j))
    in_specs = [
        pl.BlockSpec((tm, d), lambda i, j: (i, 0)),
        _layer_spec(w_in, layer, *w_block),
        pl.BlockSpec((tm, cos.shape[1]), lambda i, j: (i % n_tab, 0)),
        pl.BlockSpec((tm, cos.shape[1]), lambda i, j: (i % n_tab, 0)),
    ]
    args = [x, w_in, cos, sin]
    aliases = {}
    if prev_kv is not None:
        in_specs += [pl.BlockSpec(memory_space=pl.ANY)] * 2
        args += list(prev_kv)
        aliases = {4: 1, 5: 2}
    out_specs = [
        pl.BlockSpec((tm, seg), lambda i, j: (i, jnp.minimum(j, SEG_VB))),
        kv_spec,
        kv_spec,
        pl.BlockSpec((tm, seg), lambda i, j: (i, 0)),
    ]
    out_shape = [jax.ShapeDtypeStruct((m, SEG_G * seg), h_dtype), kv_out, kv_out,
                 jax.ShapeDtypeStruct((m, seg), F32)]
    if w_in.dtype != BF16:
        assert m == tm, "bf16 weight tiles are emitted once, by a single row tile"
        out_specs.append(pl.BlockSpec(*w_block))
        out_shape.append(jax.ShapeDtypeStruct(w_in.shape[-2:], BF16))
    return pl.pallas_call(
        functools.partial(_proj_kernel, n_alias=len(aliases), q_scale=q_scale, kb_scale=kb_scale,
                          dh_b=dh_b),
        grid=(m // tm, N_SEG),
        in_specs=in_specs,
        out_specs=out_specs,
        out_shape=out_shape,
        input_output_aliases=aliases,
        scratch_shapes=[pltpu.VMEM((tm, d), BF16)],
        compiler_params=_params(("parallel", "arbitrary")),
        name="in_proj",
    )(*args)


PAST_SPAN = 2


def _diff_lambda(lam4_ref, lam_init):
    v = lam4_ref[...]
    t1 = jnp.sum(v[0:1] * v[1:2], axis=-1, keepdims=True)
    t2 = jnp.sum(v[2:3] * v[3:4], axis=-1, keepdims=True)
    return jnp.exp(t1) - jnp.exp(t2) + lam_init


def _prompt_attention(lam, g_ref, q_ref, k_ref, v_ref, o_ref,
                      vt_ref, m1_ref, l1_ref, a1_ref, m2_ref, l2_ref, a2_ref, *, tq, lam_init):
    qi = pl.program_id(2)
    tk = vt_ref.shape[2]
    per_tile = tq // tk

    @pl.when(qi == 0)
    def _():
        for j in range(vt_ref.shape[0]):
            vt_ref[j] = v_ref[j * tk:(j + 1) * tk, :].astype(F32).T.astype(BF16)

    q = q_ref[...]
    lane = lax.broadcasted_iota(jnp.int32, q.shape, 1)
    half = q.shape[1] // 2
    zero = jnp.zeros_like(q)
    q1 = jnp.where(lane < half, q, zero)
    q2 = jnp.where(lane >= half, q, zero)

    for m_ref, l_ref, a_ref in ((m1_ref, l1_ref, a1_ref), (m2_ref, l2_ref, a2_ref)):
        m_ref[...] = jnp.full(m_ref.shape, -jnp.inf, F32)
        l_ref[...] = jnp.zeros(l_ref.shape, F32)
        a_ref[...] = jnp.zeros(a_ref.shape, F32)

    def update(s, vt, m_ref, l_ref, a_ref):
        m_old = m_ref[...]
        m_new = jnp.maximum(m_old, jnp.max(s, axis=0, keepdims=True))
        alpha = jnp.exp2(m_old - m_new)
        p = jnp.exp2(s - m_new)
        l_ref[...] = alpha * l_ref[...] + jnp.sum(p, axis=0, keepdims=True)
        a_ref[...] = alpha * a_ref[...] + _dot(vt, p.astype(BF16))
        m_ref[...] = m_new

    def block(ki, span, diag):
        start = pl.multiple_of(ki * tk, tk)
        k = k_ref[pl.ds(start, span * tk), :]
        vt = jnp.concatenate([vt_ref[ki + u] for u in range(span)], axis=1)
        s1 = _dot_nt(k, q1)
        s2 = _dot_nt(k, q2)
        if diag is not None:
            r = lax.broadcasted_iota(jnp.int32, s1.shape, 0) + diag * tk
            c = lax.broadcasted_iota(jnp.int32, s1.shape, 1)
            s1 = jnp.where(r <= c, s1, -jnp.inf)
            s2 = jnp.where(r <= c, s2, -jnp.inf)
        update(s1, vt, m1_ref, l1_ref, a1_ref)
        update(s2, vt, m2_ref, l2_ref, a2_ref)

    past = qi * per_tile

    def body(kk, carry):
        block(kk * PAST_SPAN, PAST_SPAN, None)
        return carry

    lax.fori_loop(0, past // PAST_SPAN, body, 0)
    for u in range(PAST_SPAN - 1):
        @pl.when(past % PAST_SPAN > u)
        def _():
            block(past - past % PAST_SPAN + u, 1, None)
    for j in range(per_tile):
        block(past + j, 1, j)

    ot = a1_ref[...] / l1_ref[...] - lam * (a2_ref[...] / l2_ref[...])
    ot = ot * lax.rsqrt(jnp.mean(ot * ot, axis=0, keepdims=True) + LN_EPS)
    o_ref[...] = (ot.T * g_ref[...] * (1.0 - lam_init)).astype(o_ref.dtype)


def _sample_attention(lam, g_ref, q_ref, kn_ref, vn_ref, k_refs, v_refs, o_ref, *, lam_init):
    nh, dv = q_ref.shape[1], q_ref.shape[2]
    half = dv // 2

    q8 = q_ref[0].astype(F32)
    q16 = jnp.concatenate([q8, q8], axis=0)
    row = lax.broadcasted_iota(jnp.int32, q16.shape, 0)
    lane = lax.broadcasted_iota(jnp.int32, q16.shape, 1)
    qm = jnp.where(lane // half == row // nh, q16, 0.0).astype(BF16)

    kn = kn_ref[0].astype(BF16).astype(F32)
    vn = vn_ref[0].astype(BF16).astype(F32)
    s_self = jnp.sum(qm.astype(F32) * jnp.concatenate([kn, kn], axis=0), axis=-1, keepdims=True)

    n_col = k_refs[0].shape[0] * nh
    srow = lax.broadcasted_iota(jnp.int32, (2 * nh, n_col), 0)
    scol = lax.broadcasted_iota(jnp.int32, (2 * nh, n_col), 1)
    valid = (scol & (nh - 1)) == (srow & (nh - 1))

    scores = []
    for k_ref in k_refs:
        kf = k_ref[...].reshape(n_col, dv).astype(BF16)
        scores.append(jnp.where(valid, _dot_nt(qm, kf), -jnp.inf))
    mx = functools.reduce(jnp.maximum, scores)
    m = jnp.maximum(s_self, jnp.max(mx, axis=-1, keepdims=True))
    w_self = jnp.exp2(s_self - m)
    probs = [jnp.exp2(s - m) for s in scores]
    l = w_self + jnp.sum(functools.reduce(jnp.add, probs), axis=-1, keepdims=True)
    acc = w_self * jnp.concatenate([vn, vn], axis=0)
    for p, v_ref in zip(probs, v_refs):
        acc = acc + _dot(p.astype(BF16), v_ref[...].reshape(n_col, dv).astype(BF16))

    acc = acc / l
    o = acc[0:nh] - lam * acc[nh:2 * nh]
    o_ref[0] = _rms_norm(o) * g_ref[...] * (1.0 - lam_init)


def _attn_kernel(pt_ref, lam4_ref, g_ref, q_ref, k_ref, v_ref, qd_ref, knd_ref, vnd_ref, *rest,
                 n_pages, tq, lam_init):
    kd_refs = rest[:n_pages]
    vd_refs = rest[n_pages:2 * n_pages]
    o_ref, od_ref = rest[2 * n_pages:2 * n_pages + 2]
    scratch = rest[2 * n_pages + 2:]
    lam = _diff_lambda(lam4_ref, lam_init)
    _sample_attention(lam, g_ref, qd_ref, knd_ref, vnd_ref, kd_refs, vd_refs, od_ref,
                      lam_init=lam_init)
    _prompt_attention(lam, g_ref, q_ref, k_ref, v_ref, o_ref, *scratch, tq=tq, lam_init=lam_init)


def diff_attention(h, hd3, cache_k, cache_v, pt_flat, lam4, subln_g, *,
                   layer, bsz, s_len, n_pages, tq, tk, lam_init):
    _, _, page, nh, dv = cache_k.shape
    nq = s_len // tq
    assert s_len % tq == 0 and tq % tk == 0 and nh & (nh - 1) == 0
    assert hd3.shape[0] == bsz * nh * nq, "one sample sequence per prompt (batch, head, q tile)"

    def seq(b, hh, qi):
        return (b * nh + hh) * nq + qi

    def page_spec(r):
        return pl.BlockSpec(
            (None, None, page, nh, dv),
            lambda b, hh, qi, pt: (layer, pt[seq(b, hh, qi) * n_pages + r], 0, 0, 0))

    def new_token_spec(sg):
        return pl.BlockSpec((1, nh, dv), lambda b, hh, qi, pt: (seq(b, hh, qi), sg, 0))

    stat = pltpu.VMEM((1, tq), F32)
    accs = pltpu.VMEM((dv, tq), F32)
    grid_spec = pltpu.PrefetchScalarGridSpec(
        num_scalar_prefetch=1,
        grid=(bsz, nh, nq),
        in_specs=[
            pl.BlockSpec(lam4.shape, lambda b, hh, qi, pt: (0, 0)),
            pl.BlockSpec((1, dv), lambda b, hh, qi, pt: (0, 0)),
            pl.BlockSpec((tq, dv), lambda b, hh, qi, pt: (b * nq + qi, hh)),
            pl.BlockSpec((s_len, dv), lambda b, hh, qi, pt: (b, nh + hh)),
            pl.BlockSpec((s_len, dv), lambda b, hh, qi, pt: (b, 2 * nh + hh)),
            new_token_spec(0), new_token_spec(SEG_KA), new_token_spec(SEG_VA),
        ] + [page_spec(r) for r in range(n_pages)] * 2,
        out_specs=[
            pl.BlockSpec((tq, dv), lambda b, hh, qi, pt: (b * nq + qi, hh)),
            pl.BlockSpec((1, nh, dv), lambda b, hh, qi, pt: (seq(b, hh, qi), 0, 0)),
        ],
        scratch_shapes=[pltpu.VMEM((s_len // tk, dv, tk), BF16),
                        stat, stat, accs, stat, stat, accs],
    )
    return pl.pallas_call(
        functools.partial(_attn_kernel, n_pages=n_pages, tq=tq, lam_init=lam_init),
        grid_spec=grid_spec,
        out_shape=[jax.ShapeDtypeStruct((bsz * s_len, nh * dv), BF16),
                   jax.ShapeDtypeStruct((hd3.shape[0], nh, dv), F32)],
        compiler_params=_params(("parallel", "parallel", "arbitrary")),
        name="diff_attention",
    )(pt_flat, lam4, subln_g.reshape(1, dv), h, h, h, hd3, hd3, hd3,
      *([cache_k] * n_pages), *([cache_v] * n_pages))


def _log_gamma(h):
    return float(np.log(np.float32(1.0) - np.float32(2.0) ** np.float32(-5.0 - h)))


def _ret_prompt_kernel(q_ref, k_ref, v_ref, gate_ref, r_ref, sfin_ref,
                       st_ref, dm_ref, qd_ref, kd_ref, *, n_heads, dh):
    c = pl.program_id(0)
    bsz, ch = q_ref.shape[0], q_ref.shape[1]

    @pl.when(c == 0)
    def _():
        st_ref[...] = jnp.zeros_like(st_ref)
        i = lax.broadcasted_iota(jnp.int32, (ch, ch), 0)
        jj = lax.broadcasted_iota(jnp.int32, (ch, ch), 1)
        rel = (i - jj).astype(F32)
        idx = lax.broadcasted_iota(jnp.int32, (ch, 1), 0).astype(F32)
        for h in range(n_heads):
            lg = _log_gamma(h)
            dm_ref[h] = jnp.where(rel >= 0, jnp.exp(lg * jnp.maximum(rel, 0.0)), 0.0)
            qd_ref[h] = jnp.exp(lg * (idx + 1.0))
            kd_ref[h] = jnp.exp(lg * (ch - 1.0 - idx))

    for b in range(bsz):
        for h in range(n_heads):
            cols = slice(h * dh, (h + 1) * dh)
            q = q_ref[b, :, cols]
            k = k_ref[b, :, cols]
            v = v_ref[b, :, cols]
            s0 = st_ref[b * n_heads + h]
            inner = (_dot_nt(q, k) * dm_ref[h]).astype(BF16)
            o = _dot(inner, v) + qd_ref[h] * _dot(q, s0.astype(BF16))
            kk = (k.astype(F32) * kd_ref[h]).astype(BF16)
            chunk_decay = float(np.exp(np.float32(_log_gamma(h)) * np.float32(ch)))
            st_ref[b * n_heads + h] = s0 * chunk_decay + _dot_tn(kk, v)
            r_ref[b, :, cols] = (_rms_norm(o) * _silu(gate_ref[b, :, cols])).astype(r_ref.dtype)

    @pl.when(c == pl.num_programs(0) - 1)
    def _():
        for b in range(bsz):
            for h in range(n_heads):
                sfin_ref[b, h] = st_ref[b * n_heads + h]


def ret_prompt(h3, gate3, *, n_heads, dh, chunk):
    bsz, s_len, _ = h3.shape
    width = n_heads * dh
    assert s_len % chunk == 0

    def seg_spec(sg):
        return pl.BlockSpec((bsz, chunk, width), lambda c: (0, c, sg))

    return pl.pallas_call(
        functools.partial(_ret_prompt_kernel, n_heads=n_heads, dh=dh),
        grid=(s_len // chunk,),
        in_specs=[seg_spec(SEG_QB), seg_spec(SEG_KB), seg_spec(SEG_VB), seg_spec(0)],
        out_specs=[
            pl.BlockSpec((bsz, chunk, width), lambda c: (0, c, 0)),
            pl.BlockSpec((bsz, n_heads, dh, dh), lambda c: (0, 0, 0, 0)),
        ],
        out_shape=[jax.ShapeDtypeStruct((bsz, s_len, width), BF16),
                   jax.ShapeDtypeStruct((bsz, n_heads, dh, dh), F32)],
        scratch_shapes=[
            pltpu.VMEM((bsz * n_heads, dh, dh), F32),
            pltpu.VMEM((n_heads, chunk, chunk), F32),
            pltpu.VMEM((n_heads, chunk, 1), F32),
            pltpu.VMEM((n_heads, chunk, 1), F32),
        ],
        compiler_params=_params(("arbitrary",)),
        name="ret_prompt",
    )(h3, h3, h3, gate3)


def _ret_sample_kernel(q_ref, k_ref, v_ref, gate_ref, st_ref, *rest, n_heads, dh, aliased):
    if aliased:
        rest = rest[1:]
    r_ref, snew_ref = rest
    rows = q_ref.shape[0]
    row = lax.broadcasted_iota(jnp.int32, (rows, dh), 0)

    gammas = [float(np.exp(np.float32(_log_gamma(h)))) for h in range(n_heads)]

    def body(bi, qs_acc):
        out = []
        for h in range(n_heads):
            cols = slice(h * dh, (h + 1) * dh)
            sel = row == bi
            qb = jnp.where(sel, q_ref[:, cols], 0.0).astype(BF16)
            kb = jnp.where(sel, k_ref[:, cols], 0.0).astype(BF16)
            vb = v_ref[:, cols].astype(BF16)
            s0 = st_ref[bi, h]
            snew_ref[bi, h] = s0 * gammas[h] + _dot_tn(kb, vb)
            out.append(qs_acc[h] + _dot(qb, s0.astype(BF16)))
        return tuple(out)

    qs = lax.fori_loop(0, rows, body, tuple(jnp.zeros((rows, dh), F32) for _ in range(n_heads)),
                       unroll=2)
    for h in range(n_heads):
        cols = slice(h * dh, (h + 1) * dh)
        q, k, v = (r[:, cols].astype(BF16).astype(F32) for r in (q_ref, k_ref, v_ref))
        o = jnp.sum(q * k, axis=-1, keepdims=True) * v + gammas[h] * qs[h]
        r_ref[:, cols] = _rms_norm(o) * _silu(gate_ref[:, cols])


def ret_sample(h2, gate, state, prev_out, *, layer, n_heads, dh, rows):
    bsz = h2.shape[0]
    width = n_heads * dh
    depth = state.shape[0]
    assert bsz % rows == 0
    aliased = prev_out is not None

    def seg_spec(sg):
        return pl.BlockSpec((rows, width), lambda i: (i, sg))

    st_spec = pl.BlockSpec((None, rows, n_heads, dh, dh), lambda i: (layer, i, 0, 0, 0))
    in_specs = [seg_spec(SEG_QB), seg_spec(SEG_KB), seg_spec(SEG_VB), seg_spec(0), st_spec]
    args = [h2, h2, h2, gate, state]
    if aliased:
        in_specs.append(pl.BlockSpec(memory_space=pl.ANY))
        args.append(prev_out)
    return pl.pallas_call(
        functools.partial(_ret_sample_kernel, n_heads=n_heads, dh=dh, aliased=aliased),
        grid=(bsz // rows,),
        in_specs=in_specs,
        out_specs=[pl.BlockSpec((rows, width), lambda i: (i, 0)), st_spec],
        out_shape=[jax.ShapeDtypeStruct((bsz, width), F32),
                   jax.ShapeDtypeStruct((depth, bsz, n_heads, dh, dh), F32)],
        input_output_aliases={5: 1} if aliased else {},
        compiler_params=_params(("parallel",)),
        name="ret_sample",
    )(*args)


def _out_proj_kernel(a_ref, r_ref, x_ref, w_ref, g_ref, b_ref, o_ref, *, alpha):
    wa = a_ref.shape[1]
    tm = x_ref.shape[0]
    slab = min(tm, LANES)
    for r0 in range(0, tm, slab):
        rows = slice(r0, r0 + slab)
        mixed = (_dot(a_ref[rows, :].astype(BF16), w_ref[0:wa, :])
                 + _dot(r_ref[rows, :].astype(BF16), w_ref[wa:, :]))
        o_ref[rows, :] = _layer_norm(alpha * x_ref[rows, :] + mixed, g_ref[...], b_ref[...])


def out_proj(a, r, x, w_out, g, b, *, alpha, tm):
    m, d = x.shape
    assert m % tm == 0
    return pl.pallas_call(
        functools.partial(_out_proj_kernel, alpha=alpha),
        grid=(m // tm,),
        in_specs=[
            pl.BlockSpec((tm, a.shape[1]), lambda i: (i, 0)),
            pl.BlockSpec((tm, r.shape[1]), lambda i: (i, 0)),
            pl.BlockSpec((tm, d), lambda i: (i, 0)),
            pl.BlockSpec(w_out.shape, lambda i: (0, 0)),
            pl.BlockSpec((1, d), lambda i: (0, 0)),
            pl.BlockSpec((1, d), lambda i: (0, 0)),
        ],
        out_specs=pl.BlockSpec((tm, d), lambda i: (i, 0)),
        out_shape=jax.ShapeDtypeStruct((m, d), F32),
        compiler_params=_params(("parallel",)),
        name="out_proj",
    )(a, r, x, w_out, g.reshape(1, d), b.reshape(1, d))


ROW_TILE = 512
FF_TILE = 512
RET_CHUNK = 256
F32_SUBLANES = 8


def _row_tile(m, cap):
    t = min(m, cap)
    while m % t:
        t //= 2
    return t


def kernel(x_prompt, x_sample, cache_k, cache_v, state_ret, page_table, w_in, w_out,
           lambda_q1, lambda_k1, lambda_q2, lambda_k2, subln_g,
           ffn1_w1, ffn1_w3, ffn1_w2, ffn2_w1, ffn2_w3, ffn2_w2, ln_g, ln_b):
    bsz_p, s_p, d = x_prompt.shape
    bsz_d, s_d, _ = x_sample.shape
    assert s_d == 1, "one new token per sample sequence"
    depth, _, page, n_ha, dv = cache_k.shape
    n_hb, dh_b = state_ret.shape[2], state_ret.shape[3]
    n_pages = page_table.shape[1]
    past = n_pages * page
    dqk = dv // 2
    a_width = n_ha * dv
    b_width = n_hb * dh_b
    assert w_in.shape[2] == N_SEG * a_width and a_width == b_width
    dff = ffn1_w1.shape[2]
    alpha = (2 * depth) ** 0.25

    m_p = bsz_p * s_p
    m_d = bsz_d * s_d
    xp = x_prompt.reshape(m_p, d)
    xd = x_sample.reshape(m_d, d)

    cos_p, sin_p = rope_table(s_p, dh_b // 2, 0, s_p)
    cos_d, sin_d = rope_table(m_d, dh_b // 2, past, 1)

    w_out_b = w_out.astype(BF16)
    f1 = (ffn1_w1, ffn1_w3, ffn1_w2)
    f2 = (ffn2_w1, ffn2_w3, ffn2_w2)
    pt_flat = page_table.reshape(-1)

    tf = _row_tile(dff, FF_TILE)
    tm_ffn_p, tm_ffn_d = _row_tile(m_p, ROW_TILE), _row_tile(m_d, ROW_TILE)
    tm_out_p, tm_out_d = tm_ffn_p, tm_ffn_d
    tm_proj_p, tm_proj_d = _row_tile(s_p, ROW_TILE), _row_tile(m_d, ROW_TILE)
    tq = _row_tile(s_p, ROW_TILE)
    tk = tq
    chunk = _row_tile(s_p, RET_CHUNK)
    ret_rows = F32_SUBLANES

    proj_kw = dict(depth=depth, q_scale=dqk ** -0.5 * math.log2(math.e), kb_scale=dh_b ** -0.5,
                   dh_b=dh_b)
    ns_p = []
    kv_p = kv_d = ns_d = None
    ffn1_w = f1
    for l in range(depth):
        lam_init = 0.8 - 0.6 * math.exp(-0.3 * l)
        lam4 = jnp.stack([lambda_q1[l], lambda_k1[l], lambda_q2[l], lambda_k2[l]]).astype(F32)
        ffn_kw = dict(layer=l, alpha=alpha, tf=tf)

        xd, *emitted = ffn_half(xd, *ffn1_w, ln_g[l, 0], ln_b[l, 0], tm=tm_ffn_d, **ffn_kw)
        ffn1_b = emitted or ffn1_w
        hd, *kv_d, gf, w_in_b = in_proj(xd, w_in, cos_d, sin_d, kv_d, layer=l, tm=tm_proj_d,
                                        h_dtype=F32, **proj_kw)
        xp, *ffn2_b = ffn_half(xp, *ffn1_b, ln_g[l, 0], ln_b[l, 0], tm=tm_ffn_p, cast=f2,
                               cast_layer=l, **ffn_kw)
        hp, *kv_p, gf_p = in_proj(xp, w_in_b, cos_p, sin_p, kv_p, layer=l, tm=tm_proj_p,
                                  h_dtype=BF16, **proj_kw)

        attn_p, attn_d = diff_attention(hp, hd.reshape(m_d, SEG_G * n_ha, dv), cache_k, cache_v, pt_flat,
                                        lam4, subln_g[l], layer=l, bsz=bsz_p, s_len=s_p,
                                        n_pages=n_pages, tq=tq, tk=tk, lam_init=lam_init)

        ret, ns_d = ret_sample(hd, gf, state_ret, ns_d, layer=l, n_heads=n_hb, dh=dh_b, rows=ret_rows)
        xd = out_proj(attn_d.reshape(m_d, a_width), ret, xd, w_out_b[l], ln_g[l, 1], ln_b[l, 1],
                      alpha=alpha, tm=tm_out_d)
        xd, = ffn_half(xd, *ffn2_b, ln_g[l, 2], ln_b[l, 2], tm=tm_ffn_d, **ffn_kw)

        ret, s_fin = ret_prompt(hp.reshape(bsz_p, s_p, -1), gf_p.reshape(bsz_p, s_p, -1),
                                n_heads=n_hb, dh=dh_b, chunk=chunk)
        xp = out_proj(attn_p, ret.reshape(m_p, b_width), xp, w_out_b[l], ln_g[l, 1], ln_b[l, 1],
                      alpha=alpha, tm=tm_out_p)
        nxt = dict(cast=f1, cast_layer=l + 1) if l + 1 < depth else {}
        xp, *ffn1_w = ffn_half(xp, *ffn2_b, ln_g[l, 2], ln_b[l, 2], tm=tm_ffn_p, **nxt, **ffn_kw)
        ns_p.append(s_fin)

    kv_shape_p = (depth, bsz_p, s_p, n_ha, dv)
    kv_shape_d = (depth, bsz_d, s_d, n_ha, dv)
    return (xp.reshape(bsz_p, s_p, d), xd.reshape(bsz_d, s_d, d),
            kv_p[0].reshape(kv_shape_p), kv_p[1].reshape(kv_shape_p), jnp.stack(ns_p),
            kv_d[0].reshape(kv_shape_d), kv_d[1].reshape(kv_shape_d), ns_d)
```

```python
import functools
import math

import numpy as np
import jax
import jax.numpy as jnp
from jax import lax
from jax.experimental import pallas as pl
from jax.experimental.pallas import tpu as pltpu

F32 = jnp.float32
BF16 = jnp.bfloat16

LN_EPS = 1e-5
ROPE_BASE = 10000.0
LANES = 128
VMEM_LIMIT = 56 * 1024 * 1024


def _params(sem, vmem=VMEM_LIMIT):
    return pltpu.CompilerParams(dimension_semantics=sem, vmem_limit_bytes=vmem)


def _layer_norm(y, g, b):
    mu = jnp.mean(y, axis=-1, keepdims=True)
    d = y - mu
    var = jnp.mean(d * d, axis=-1, keepdims=True)
    return d * lax.rsqrt(var + LN_EPS) * g + b


def _rms_norm(y):
    return y * lax.rsqrt(jnp.mean(y * y, axis=-1, keepdims=True) + LN_EPS)


def _silu(a):
    return a * (1.0 / (1.0 + jnp.exp(-a)))


def _dot_nt(a, b):
    return lax.dot_general(a, b, (((1,), (1,)), ((), ())), preferred_element_type=F32)


def _dot_tn(a, b):
    return lax.dot_general(a, b, (((0,), (0,)), ((), ())), preferred_element_type=F32)


def _dot(a, b):
    return jnp.dot(a, b, preferred_element_type=F32)


def _rope_kernel(inv_ref, cos_ref, sin_ref, *, base, period):
    rows = cos_ref.shape[0]
    row = lax.broadcasted_iota(jnp.int32, cos_ref.shape, 0)
    if period == 1:
        pos = jnp.full(cos_ref.shape, base, jnp.int32)
    else:
        assert period == rows
        pos = row + base
    ang = pos.astype(F32) * inv_ref[...]
    cos_ref[...] = jnp.cos(ang)
    sin_ref[...] = jnp.sin(ang)


def rope_table(rows, half, base, period):
    inv = (ROPE_BASE ** (-jnp.arange(half, dtype=F32) / half)).reshape(1, half)
    return pl.pallas_call(
        functools.partial(_rope_kernel, base=base, period=period),
        out_shape=(jax.ShapeDtypeStruct((rows, half), F32),) * 2,
        name="rope_table",
    )(inv)


def _layer_spec(w, layer, block, index_map):
    if w.ndim == 2:
        return pl.BlockSpec(block, index_map)
    return pl.BlockSpec((None,) + block, lambda *g: (layer,) + index_map(*g))


def _ffn_kernel(x_ref, *refs, alpha, cast_x):
    xb_ref, refs = (refs[-1], refs[:-1]) if cast_x else (refs[0], refs[1:])
    w1_ref, w3_ref, w2_ref, g_ref, b_ref, o_ref, ob_ref = refs[:7]
    emit_out = refs[7:]
    j = pl.program_id(1)

    @pl.when(j == 0)
    def _():
        o_ref[...] = jnp.zeros_like(o_ref)
        if cast_x:
            xb_ref[...] = x_ref[...].astype(BF16)

    w1, w3, w2 = (r[...].astype(BF16) for r in (w1_ref, w3_ref, w2_ref))
    for dst, w in zip(emit_out, (w1, w3, w2)):
        dst[...] = w

    xb = xb_ref[...]
    a = _dot(xb, w1)
    b = _dot(xb, w3)
    h = (_silu(a) * b).astype(BF16)
    o_ref[...] += _dot(h, w2)

    @pl.when(j == pl.num_programs(1) - 1)
    def _():
        y = alpha * x_ref[...] + 0.5 * o_ref[...]
        y = _layer_norm(y, g_ref[...], b_ref[...])
        o_ref[...] = y
        ob_ref[...] = y.astype(BF16)


def ffn_half(x, xb, w1, w3, w2, g, b, *, layer, alpha, tm, tf):
    m, d = x.shape
    cast_x = xb is None
    dff = w1.shape[-1]
    assert m % tm == 0 and dff % tf == 0
    up = ((d, tf), lambda i, j: (0, j))
    down = ((tf, d), lambda i, j: (j, 0))
    rows = pl.BlockSpec((tm, d), lambda i, j: (i, 0))
    out_specs = [rows, rows]
    out_shape = [jax.ShapeDtypeStruct((m, d), F32), jax.ShapeDtypeStruct((m, d), BF16)]
    if w1.dtype != BF16:
        assert m == tm, "bf16 weight tiles are emitted once, by a single row tile"
        out_specs += [pl.BlockSpec(*up), pl.BlockSpec(*up), pl.BlockSpec(*down)]
        out_shape += [jax.ShapeDtypeStruct(w.shape[-2:], BF16) for w in (w1, w3, w2)]
    return pl.pallas_call(
        functools.partial(_ffn_kernel, alpha=alpha, cast_x=cast_x),
        grid=(m // tm, dff // tf),
        in_specs=[rows] * (1 if cast_x else 2) + [
            _layer_spec(w1, layer, *up),
            _layer_spec(w3, layer, *up),
            _layer_spec(w2, layer, *down),
            pl.BlockSpec((1, d), lambda i, j: (0, 0)),
            pl.BlockSpec((1, d), lambda i, j: (0, 0)),
        ],
        out_specs=out_specs,
        out_shape=out_shape,
        scratch_shapes=[pltpu.VMEM((tm, d), BF16)] if cast_x else [],
        compiler_params=_params(("parallel", "arbitrary")),
        name="ffn_half",
    )(*([x] if cast_x else [x, xb]), w1, w3, w2, g.reshape(1, d), b.reshape(1, d))


N_SEG = 7
SEG_KA, SEG_VA, SEG_QB, SEG_KB, SEG_VB, SEG_G = 1, 2, 3, 4, 5, 6


def _proj_kernel(xb_ref, w_ref, cos_ref, sin_ref, *rest, n_alias, q_scale, kb_scale, dh_b):
    h_ref, kf_ref, vf_ref, gf_ref = rest[n_alias:n_alias + 4]
    j = pl.program_id(1)

    w = w_ref[...].astype(BF16)
    for dst in rest[n_alias + 4:]:
        dst[...] = w
    acc = _dot(xb_ref[...], w)
    hd = h_ref.dtype

    @pl.when(j == 0)
    def _():
        h_ref[...] = (acc * q_scale).astype(hd)

    @pl.when(j == SEG_KA)
    def _():
        kf_ref[...] = acc
        h_ref[...] = acc.astype(hd)

    @pl.when(j == SEG_VA)
    def _():
        vf_ref[...] = acc
        h_ref[...] = acc.astype(hd)

    def rotary(scale):
        cos = cos_ref[...]
        sin = sin_ref[...]
        half = dh_b // 2
        for h in range(acc.shape[1] // dh_b):
            x1 = acc[:, h * dh_b:h * dh_b + half]
            x2 = acc[:, h * dh_b + half:(h + 1) * dh_b]
            h_ref[:, h * dh_b:h * dh_b + half] = ((x1 * cos - x2 * sin) * scale).astype(hd)
            h_ref[:, h * dh_b + half:(h + 1) * dh_b] = ((x2 * cos + x1 * sin) * scale).astype(hd)

    @pl.when(j == SEG_QB)
    def _():
        rotary(1.0)

    @pl.when(j == SEG_KB)
    def _():
        rotary(kb_scale)

    @pl.when(j == SEG_VB)
    def _():
        h_ref[...] = acc.astype(hd)

    @pl.when(j == SEG_G)
    def _():
        gf_ref[...] = acc


def in_proj(x, w_in, cos, sin, prev_kv, *, layer, depth, tm, h_dtype, q_scale, kb_scale, dh_b):
    m, d = x.shape
    seg = w_in.shape[-1] // N_SEG
    n_tab = cos.shape[0] // tm
    assert m % tm == 0 and cos.shape[0] % tm == 0
    kv_out = jax.ShapeDtypeStruct((depth, m, seg), F32)
    kv_spec = pl.BlockSpec((None, tm, seg), lambda i, j: (layer, i, 0))
    w_block = ((d, seg), lambda i, j: (0, j))
    in_specs = [
        pl.BlockSpec((tm, d), lambda i, j: (i, 0)),
        _layer_spec(w_in, layer, *w_block),
        pl.BlockSpec((tm, cos.shape[1]), lambda i, j: (i % n_tab, 0)),
        pl.BlockSpec((tm, cos.shape[1]), lambda i, j: (i % n_tab, 0)),
    ]
    args = [x, w_in, cos, sin]
    aliases = {}
    if prev_kv is not None:
        in_specs += [pl.BlockSpec(memory_space=pl.ANY)] * 2
        args += list(prev_kv)
        aliases = {4: 1, 5: 2}
    out_specs = [
        pl.BlockSpec((tm, seg), lambda i, j: (i, jnp.minimum(j, SEG_VB))),
        kv_spec,
        kv_spec,
        pl.BlockSpec((tm, seg), lambda i, j: (i, 0)),
    ]
    out_shape = [jax.ShapeDtypeStruct((m, SEG_G * seg), h_dtype), kv_out, kv_out,
                 jax.ShapeDtypeStruct((m, seg), F32)]
    if w_in.dtype != BF16:
        assert m == tm, "bf16 weight tiles are emitted once, by a single row tile"
        out_specs.append(pl.BlockSpec(*w_block))
        out_shape.append(jax.ShapeDtypeStruct(w_in.shape[-2:], BF16))
    return pl.pallas_call(
        functools.partial(_proj_kernel, n_alias=len(aliases), q_scale=q_scale, kb_scale=kb_scale,
                          dh_b=dh_b),
        grid=(m // tm, N_SEG),
        in_specs=in_specs,
        out_specs=out_specs,
        out_shape=out_shape,
        input_output_aliases=aliases,
        compiler_params=_params(("parallel", "arbitrary")),
        name="in_proj",
    )(*args)


PAST_SPAN = 2


def _diff_lambda(lam4_ref, lam_init):
    v = lam4_ref[...]
    t1 = jnp.sum(v[0:1] * v[1:2], axis=-1, keepdims=True)
    t2 = jnp.sum(v[2:3] * v[3:4], axis=-1, keepdims=True)
    return jnp.exp(t1) - jnp.exp(t2) + lam_init


def _prompt_attention(lam, g_ref, q_ref, k_ref, v_ref, o_ref,
                      vt_ref, m1_ref, l1_ref, a1_ref, m2_ref, l2_ref, a2_ref, *, tq, lam_init):
    qi = pl.program_id(2)
    tk = vt_ref.shape[2]
    per_tile = tq // tk

    @pl.when(qi == 0)
    def _():
        for j in range(vt_ref.shape[0]):
            vt_ref[j] = v_ref[j * tk:(j + 1) * tk, :].astype(F32).T.astype(BF16)

    q = q_ref[...]
    lane = lax.broadcasted_iota(jnp.int32, q.shape, 1)
    half = q.shape[1] // 2
    zero = jnp.zeros_like(q)
    q1 = jnp.where(lane < half, q, zero)
    q2 = jnp.where(lane >= half, q, zero)

    for m_ref, l_ref, a_ref in ((m1_ref, l1_ref, a1_ref), (m2_ref, l2_ref, a2_ref)):
        m_ref[...] = jnp.full(m_ref.shape, -jnp.inf, F32)
        l_ref[...] = jnp.zeros(l_ref.shape, F32)
        a_ref[...] = jnp.zeros(a_ref.shape, F32)

    def update(s, vt, m_ref, l_ref, a_ref):
        m_old = m_ref[...]
        m_new = jnp.maximum(m_old, jnp.max(s, axis=0, keepdims=True))
        alpha = jnp.exp2(m_old - m_new)
        p = jnp.exp2(s - m_new)
        l_ref[...] = alpha * l_ref[...] + jnp.sum(p, axis=0, keepdims=True)
        a_ref[...] = alpha * a_ref[...] + _dot(vt, p.astype(BF16))
        m_ref[...] = m_new

    def block(ki, span, diag):
        start = pl.multiple_of(ki * tk, tk)
        k = k_ref[pl.ds(start, span * tk), :]
        vt = jnp.concatenate([vt_ref[ki + u] for u in range(span)], axis=1)
        s1 = _dot_nt(k, q1)
        s2 = _dot_nt(k, q2)
        if diag is not None:
            r = lax.broadcasted_iota(jnp.int32, s1.shape, 0) + diag * tk
            c = lax.broadcasted_iota(jnp.int32, s1.shape, 1)
            s1 = jnp.where(r <= c, s1, -jnp.inf)
            s2 = jnp.where(r <= c, s2, -jnp.inf)
        update(s1, vt, m1_ref, l1_ref, a1_ref)
        update(s2, vt, m2_ref, l2_ref, a2_ref)

    past = qi * per_tile

    def body(kk, carry):
        block(kk * PAST_SPAN, PAST_SPAN, None)
        return carry

    lax.fori_loop(0, past // PAST_SPAN, body, 0)
    for u in range(PAST_SPAN - 1):
        @pl.when(past % PAST_SPAN > u)
        def _():
            block(past - past % PAST_SPAN + u, 1, None)
    for j in range(per_tile):
        block(past + j, 1, j)

    ot = a1_ref[...] / l1_ref[...] - lam * (a2_ref[...] / l2_ref[...])
    ot = ot * lax.rsqrt(jnp.mean(ot * ot, axis=0, keepdims=True) + LN_EPS)
    o_ref[...] = (ot.T * g_ref[...] * (1.0 - lam_init)).astype(o_ref.dtype)


def _sample_attention(lam, g_ref, q_ref, kn_ref, vn_ref, k_refs, v_refs, o_ref, *, lam_init):
    nh, dv = q_ref.shape[1], q_ref.shape[2]
    half = dv // 2

    q8 = q_ref[0].astype(F32)
    q16 = jnp.concatenate([q8, q8], axis=0)
    row = lax.broadcasted_iota(jnp.int32, q16.shape, 0)
    lane = lax.broadcasted_iota(jnp.int32, q16.shape, 1)
    qm = jnp.where(lane // half == row // nh, q16, 0.0).astype(BF16)

    kn = kn_ref[0].astype(BF16).astype(F32)
    vn = vn_ref[0].astype(BF16).astype(F32)
    s_self = jnp.sum(qm.astype(F32) * jnp.concatenate([kn, kn], axis=0), axis=-1, keepdims=True)

    n_col = k_refs[0].shape[0] * nh
    srow = lax.broadcasted_iota(jnp.int32, (2 * nh, n_col), 0)
    scol = lax.broadcasted_iota(jnp.int32, (2 * nh, n_col), 1)
    valid = (scol & (nh - 1)) == (srow & (nh - 1))

    scores = []
    for k_ref in k_refs:
        kf = k_ref[...].reshape(n_col, dv).astype(BF16)
        scores.append(jnp.where(valid, _dot_nt(qm, kf), -jnp.inf))
    mx = functools.reduce(jnp.maximum, scores)
    m = jnp.maximum(s_self, jnp.max(mx, axis=-1, keepdims=True))
    w_self = jnp.exp2(s_self - m)
    probs = [jnp.exp2(s - m) for s in scores]
    l = w_self + jnp.sum(functools.reduce(jnp.add, probs), axis=-1, keepdims=True)
    acc = w_self * jnp.concatenate([vn, vn], axis=0)
    for p, v_ref in zip(probs, v_refs):
        acc = acc + _dot(p.astype(BF16), v_ref[...].reshape(n_col, dv).astype(BF16))

    acc = acc / l
    o = acc[0:nh] - lam * acc[nh:2 * nh]
    o_ref[0] = _rms_norm(o) * g_ref[...] * (1.0 - lam_init)


def _attn_kernel(pt_ref, lam4_ref, g_ref, q_ref, k_ref, v_ref, qd_ref, knd_ref, vnd_ref, *rest,
                 n_pages, tq, lam_init):
    kd_refs = rest[:n_pages]
    vd_refs = rest[n_pages:2 * n_pages]
    o_ref, od_ref = rest[2 * n_pages:2 * n_pages + 2]
    scratch = rest[2 * n_pages + 2:]
    lam = _diff_lambda(lam4_ref, lam_init)
    _sample_attention(lam, g_ref, qd_ref, knd_ref, vnd_ref, kd_refs, vd_refs, od_ref,
                      lam_init=lam_init)
    _prompt_attention(lam, g_ref, q_ref, k_ref, v_ref, o_ref, *scratch, tq=tq, lam_init=lam_init)


def diff_attention(h, hd3, cache_k, cache_v, pt_flat, lam4, subln_g, *,
                   layer, bsz, s_len, n_pages, tq, tk, lam_init):
    _, _, page, nh, dv = cache_k.shape
    nq = s_len // tq
    assert s_len % tq == 0 and tq % tk == 0 and nh & (nh - 1) == 0
    assert hd3.shape[0] == bsz * nh * nq, "one sample sequence per prompt (batch, head, q tile)"

    def seq(b, hh, qi):
        return (b * nh + hh) * nq + qi

    def page_spec(r):
        return pl.BlockSpec(
            (None, None, page, nh, dv),
            lambda b, hh, qi, pt: (layer, pt[seq(b, hh, qi) * n_pages + r], 0, 0, 0))

    def new_token_spec(sg):
        return pl.BlockSpec((1, nh, dv), lambda b, hh, qi, pt: (seq(b, hh, qi), sg, 0))

    stat = pltpu.VMEM((1, tq), F32)
    accs = pltpu.VMEM((dv, tq), F32)
    grid_spec = pltpu.PrefetchScalarGridSpec(
        num_scalar_prefetch=1,
        grid=(bsz, nh, nq),
        in_specs=[
            pl.BlockSpec(lam4.shape, lambda b, hh, qi, pt: (0, 0)),
            pl.BlockSpec((1, dv), lambda b, hh, qi, pt: (0, 0)),
            pl.BlockSpec((tq, dv), lambda b, hh, qi, pt: (b * nq + qi, hh)),
            pl.BlockSpec((s_len, dv), lambda b, hh, qi, pt: (b, nh + hh)),
            pl.BlockSpec((s_len, dv), lambda b, hh, qi, pt: (b, 2 * nh + hh)),
            new_token_spec(0), new_token_spec(SEG_KA), new_token_spec(SEG_VA),
        ] + [page_spec(r) for r in range(n_pages)] * 2,
        out_specs=[
            pl.BlockSpec((tq, dv), lambda b, hh, qi, pt: (b * nq + qi, hh)),
            pl.BlockSpec((1, nh, dv), lambda b, hh, qi, pt: (seq(b, hh, qi), 0, 0)),
        ],
        scratch_shapes=[pltpu.VMEM((s_len // tk, dv, tk), BF16),
                        stat, stat, accs, stat, stat, accs],
    )
    return pl.pallas_call(
        functools.partial(_attn_kernel, n_pages=n_pages, tq=tq, lam_init=lam_init),
        grid_spec=grid_spec,
        out_shape=[jax.ShapeDtypeStruct((bsz * s_len, nh * dv), BF16),
                   jax.ShapeDtypeStruct((hd3.shape[0], nh, dv), F32)],
        compiler_params=_params(("parallel", "parallel", "arbitrary")),
        name="diff_attention",
    )(pt_flat, lam4, subln_g.reshape(1, dv), h, h, h, hd3, hd3, hd3,
      *([cache_k] * n_pages), *([cache_v] * n_pages))


def _log_gamma(h):
    return float(np.log(np.float32(1.0) - np.float32(2.0) ** np.float32(-5.0 - h)))


def _ret_prompt_kernel(q_ref, k_ref, v_ref, gate_ref, after_ref, r_ref, sfin_ref,
                       st_ref, dm_ref, qd_ref, kd_ref, *, n_heads, dh):
    c = pl.program_id(0)
    bsz, ch = q_ref.shape[0], q_ref.shape[1]

    @pl.when(c == 0)
    def _():
        st_ref[...] = jnp.zeros_like(st_ref)
        i = lax.broadcasted_iota(jnp.int32, (ch, ch), 0)
        jj = lax.broadcasted_iota(jnp.int32, (ch, ch), 1)
        rel = (i - jj).astype(F32)
        idx = lax.broadcasted_iota(jnp.int32, (ch, 1), 0).astype(F32)
        for h in range(n_heads):
            lg = _log_gamma(h)
            dm_ref[h] = jnp.where(rel >= 0, jnp.exp(lg * jnp.maximum(rel, 0.0)), 0.0)
            qd_ref[h] = jnp.exp(lg * (idx + 1.0))
            kd_ref[h] = jnp.exp(lg * (ch - 1.0 - idx))

    for b in range(bsz):
        for h in range(n_heads):
            cols = slice(h * dh, (h + 1) * dh)
            q = q_ref[b, :, cols]
            k = k_ref[b, :, cols]
            v = v_ref[b, :, cols]
            s0 = st_ref[b * n_heads + h]
            inner = (_dot_nt(q, k) * dm_ref[h]).astype(BF16)
            o = _dot(inner, v) + qd_ref[h] * _dot(q, s0.astype(BF16))
            kk = (k.astype(F32) * kd_ref[h]).astype(BF16)
            chunk_decay = float(np.exp(np.float32(_log_gamma(h)) * np.float32(ch)))
            st_ref[b * n_heads + h] = s0 * chunk_decay + _dot_tn(kk, v)
            r_ref[b, :, cols] = (_rms_norm(o) * _silu(gate_ref[b, :, cols])).astype(r_ref.dtype)

    @pl.when(c == pl.num_programs(0) - 1)
    def _():
        for b in range(bsz):
            for h in range(n_heads):
                sfin_ref[b, h] = st_ref[b * n_heads + h]


def ret_prompt(h3, gate3, after, *, n_heads, dh, chunk):
    bsz, s_len, _ = h3.shape
    width = n_heads * dh
    assert s_len % chunk == 0

    def seg_spec(sg):
        return pl.BlockSpec((bsz, chunk, width), lambda c: (0, c, sg))

    return pl.pallas_call(
        functools.partial(_ret_prompt_kernel, n_heads=n_heads, dh=dh),
        grid=(s_len // chunk,),
        in_specs=[seg_spec(SEG_QB), seg_spec(SEG_KB), seg_spec(SEG_VB), seg_spec(0),
                  pl.BlockSpec(memory_space=pl.ANY)],
        out_specs=[
            pl.BlockSpec((bsz, chunk, width), lambda c: (0, c, 0)),
            pl.BlockSpec((bsz, n_heads, dh, dh), lambda c: (0, 0, 0, 0)),
        ],
        out_shape=[jax.ShapeDtypeStruct((bsz, s_len, width), BF16),
                   jax.ShapeDtypeStruct((bsz, n_heads, dh, dh), F32)],
        scratch_shapes=[
            pltpu.VMEM((bsz * n_heads, dh, dh), F32),
            pltpu.VMEM((n_heads, chunk, chunk), F32),
            pltpu.VMEM((n_heads, chunk, 1), F32),
            pltpu.VMEM((n_heads, chunk, 1), F32),
        ],
        compiler_params=_params(("arbitrary",)),
        name="ret_prompt",
    )(h3, h3, h3, gate3, after)


def _ret_sample_kernel(q_ref, k_ref, v_ref, gate_ref, st_ref, after_ref, *rest, n_heads, dh):
    r_ref, snew_ref = rest[-2:]
    rows = q_ref.shape[0]
    row = lax.broadcasted_iota(jnp.int32, (rows, dh), 0)

    gammas = [float(np.exp(np.float32(_log_gamma(h)))) for h in range(n_heads)]

    def body(bi, qs_acc):
        out = []
        for h in range(n_heads):
            cols = slice(h * dh, (h + 1) * dh)
            sel = row == bi
            qb = jnp.where(sel, q_ref[:, cols], 0.0).astype(BF16)
            kb = jnp.where(sel, k_ref[:, cols], 0.0).astype(BF16)
            vb = v_ref[:, cols].astype(BF16)
            s0 = st_ref[bi, h]
            snew_ref[bi, h] = s0 * gammas[h] + _dot_tn(kb, vb)
            out.append(qs_acc[h] + _dot(qb, s0.astype(BF16)))
        return tuple(out)

    qs = lax.fori_loop(0, rows, body, tuple(jnp.zeros((rows, dh), F32) for _ in range(n_heads)),
                       unroll=2)
    for h in range(n_heads):
        cols = slice(h * dh, (h + 1) * dh)
        q, k, v = (r[:, cols].astype(BF16).astype(F32) for r in (q_ref, k_ref, v_ref))
        o = jnp.sum(q * k, axis=-1, keepdims=True) * v + gammas[h] * qs[h]
        r_ref[:, cols] = _rms_norm(o) * _silu(gate_ref[:, cols])


def ret_sample(h2, gate, state, after, prev_out, *, layer, n_heads, dh, rows):
    bsz = h2.shape[0]
    width = n_heads * dh
    depth = state.shape[0]
    assert bsz % rows == 0
    aliased = prev_out is not None

    def seg_spec(sg):
        return pl.BlockSpec((rows, width), lambda i: (i, sg))

    st_spec = pl.BlockSpec((None, rows, n_heads, dh, dh), lambda i: (layer, i, 0, 0, 0))
    in_specs = [seg_spec(SEG_QB), seg_spec(SEG_KB), seg_spec(SEG_VB), seg_spec(0), st_spec,
                pl.BlockSpec(memory_space=pl.ANY)]
    args = [h2, h2, h2, gate, state, after]
    if aliased:
        in_specs.append(pl.BlockSpec(memory_space=pl.ANY))
        args.append(prev_out)
    return pl.pallas_call(
        functools.partial(_ret_sample_kernel, n_heads=n_heads, dh=dh),
        grid=(bsz // rows,),
        in_specs=in_specs,
        out_specs=[pl.BlockSpec((rows, width), lambda i: (i, 0)), st_spec],
        out_shape=[jax.ShapeDtypeStruct((bsz, width), F32),
                   jax.ShapeDtypeStruct((depth, bsz, n_heads, dh, dh), F32)],
        input_output_aliases={6: 1} if aliased else {},
        compiler_params=_params(("parallel",)),
        name="ret_sample",
    )(*args)


def _out_proj_kernel(a_ref, r_ref, x_ref, w_ref, g_ref, b_ref, o_ref, ob_ref, *, alpha):
    wa = a_ref.shape[1]
    tm = x_ref.shape[0]
    slab = min(tm, LANES)
    for r0 in range(0, tm, slab):
        rows = slice(r0, r0 + slab)
        mixed = (_dot(a_ref[rows, :].astype(BF16), w_ref[0:wa, :])
                 + _dot(r_ref[rows, :].astype(BF16), w_ref[wa:, :]))
        y = _layer_norm(alpha * x_ref[rows, :] + mixed, g_ref[...], b_ref[...])
        o_ref[rows, :] = y
        ob_ref[rows, :] = y.astype(BF16)


def out_proj(a, r, x, w_out, g, b, *, alpha, tm):
    m, d = x.shape
    assert m % tm == 0
    return pl.pallas_call(
        functools.partial(_out_proj_kernel, alpha=alpha),
        grid=(m // tm,),
        in_specs=[
            pl.BlockSpec((tm, a.shape[1]), lambda i: (i, 0)),
            pl.BlockSpec((tm, r.shape[1]), lambda i: (i, 0)),
            pl.BlockSpec((tm, d), lambda i: (i, 0)),
            pl.BlockSpec(w_out.shape, lambda i: (0, 0)),
            pl.BlockSpec((1, d), lambda i: (0, 0)),
            pl.BlockSpec((1, d), lambda i: (0, 0)),
        ],
        out_specs=[pl.BlockSpec((tm, d), lambda i: (i, 0))] * 2,
        out_shape=[jax.ShapeDtypeStruct((m, d), F32), jax.ShapeDtypeStruct((m, d), BF16)],
        compiler_params=_params(("parallel",)),
        name="out_proj",
    )(a, r, x, w_out, g.reshape(1, d), b.reshape(1, d))


ROW_TILE = 512
FF_TILE = 512
RET_CHUNK = 256
F32_SUBLANES = 8


def _row_tile(m, cap):
    t = min(m, cap)
    while m % t:
        t //= 2
    return t


def kernel(x_prompt, x_sample, cache_k, cache_v, state_ret, page_table, w_in, w_out,
           lambda_q1, lambda_k1, lambda_q2, lambda_k2, subln_g,
           ffn1_w1, ffn1_w3, ffn1_w2, ffn2_w1, ffn2_w3, ffn2_w2, ln_g, ln_b):
    bsz_p, s_p, d = x_prompt.shape
    bsz_d, s_d, _ = x_sample.shape
    assert s_d == 1, "one new token per sample sequence"
    depth, _, page, n_ha, dv = cache_k.shape
    n_hb, dh_b = state_ret.shape[2], state_ret.shape[3]
    n_pages = page_table.shape[1]
    past = n_pages * page
    dqk = dv // 2
    a_width = n_ha * dv
    b_width = n_hb * dh_b
    assert w_in.shape[2] == N_SEG * a_width and a_width == b_width
    dff = ffn1_w1.shape[2]
    alpha = (2 * depth) ** 0.25

    m_p = bsz_p * s_p
    m_d = bsz_d * s_d
    xp = x_prompt.reshape(m_p, d)
    xd = x_sample.reshape(m_d, d)

    cos_p, sin_p = rope_table(s_p, dh_b // 2, 0, s_p)
    cos_d, sin_d = rope_table(m_d, dh_b // 2, past, 1)

    w_out_b = w_out.astype(BF16)
    f1 = (ffn1_w1, ffn1_w3, ffn1_w2)
    f2 = (ffn2_w1, ffn2_w3, ffn2_w2)
    pt_flat = page_table.reshape(-1)

    tf = _row_tile(dff, FF_TILE)
    tm_ffn_p, tm_ffn_d = _row_tile(m_p, ROW_TILE), _row_tile(m_d, ROW_TILE)
    tm_out_p, tm_out_d = tm_ffn_p, tm_ffn_d
    tm_proj_p, tm_proj_d = _row_tile(s_p, 2 * ROW_TILE), _row_tile(m_d, ROW_TILE)
    tq = _row_tile(s_p, ROW_TILE)
    tk = tq
    chunk = _row_tile(s_p, RET_CHUNK)
    ret_rows = F32_SUBLANES

    proj_kw = dict(depth=depth, q_scale=dqk ** -0.5 * math.log2(math.e), kb_scale=dh_b ** -0.5,
                   dh_b=dh_b)
    ns_p = []
    kv_p = kv_d = ns_d = None
    xp_b = xd_b = None
    for l in range(depth):
        lam_init = 0.8 - 0.6 * math.exp(-0.3 * l)
        lam4 = jnp.stack([lambda_q1[l], lambda_k1[l], lambda_q2[l], lambda_k2[l]]).astype(F32)
        ffn_kw = dict(layer=l, alpha=alpha, tf=tf)

        xd, xd_b, *ffn1_b = ffn_half(xd, xd_b, *f1, ln_g[l, 0], ln_b[l, 0], tm=tm_ffn_d, **ffn_kw)
        hd, *kv_d, gf, w_in_b = in_proj(xd_b, w_in, cos_d, sin_d, kv_d, layer=l, tm=tm_proj_d,
                                        h_dtype=F32, **proj_kw)
        xp, xp_b = ffn_half(xp, xp_b, *ffn1_b, ln_g[l, 0], ln_b[l, 0], tm=tm_ffn_p, **ffn_kw)
        hp, *kv_p, gf_p = in_proj(xp_b, w_in_b, cos_p, sin_p, kv_p, layer=l, tm=tm_proj_p,
                                  h_dtype=BF16, **proj_kw)

        attn_p, attn_d = diff_attention(hp, hd.reshape(m_d, SEG_G * n_ha, dv), cache_k, cache_v, pt_flat,
                                        lam4, subln_g[l], layer=l, bsz=bsz_p, s_len=s_p,
                                        n_pages=n_pages, tq=tq, tk=tk, lam_init=lam_init)

        ret, ns_d = ret_sample(hd, gf, state_ret, attn_d, ns_d, layer=l, n_heads=n_hb, dh=dh_b,
                               rows=ret_rows)
        xd, xd_b = out_proj(attn_d.reshape(m_d, a_width), ret, xd, w_out_b[l], ln_g[l, 1], ln_b[l, 1],
                            alpha=alpha, tm=tm_out_d)
        xd, xd_b, *ffn2_b = ffn_half(xd, xd_b, *f2, ln_g[l, 2], ln_b[l, 2], tm=tm_ffn_d, **ffn_kw)

        ret, s_fin = ret_prompt(hp.reshape(bsz_p, s_p, -1), gf_p.reshape(bsz_p, s_p, -1), attn_p,
                                n_heads=n_hb, dh=dh_b, chunk=chunk)
        xp, xp_b = out_proj(attn_p, ret.reshape(m_p, b_width), xp, w_out_b[l], ln_g[l, 1], ln_b[l, 1],
                            alpha=alpha, tm=tm_out_p)
        xp, xp_b = ffn_half(xp, xp_b, *ffn2_b, ln_g[l, 2], ln_b[l, 2], tm=tm_ffn_p, **ffn_kw)
        ns_p.append(s_fin)

    kv_shape_p = (depth, bsz_p, s_p, n_ha, dv)
    kv_shape_d = (depth, bsz_d, s_d, n_ha, dv)
    return (xp.reshape(bsz_p, s_p, d), xd.reshape(bsz_d, s_d, d),
            kv_p[0].reshape(kv_shape_p), kv_p[1].reshape(kv_shape_p), jnp.stack(ns_p),
            kv_d[0].reshape(kv_shape_d), kv_d[1].reshape(kv_shape_d), ns_d)
```

```python
import functools
import math

import numpy as np
import jax
import jax.numpy as jnp
from jax import lax
from jax.experimental import pallas as pl
from jax.experimental.pallas import tpu as pltpu

F32 = jnp.float32
BF16 = jnp.bfloat16

LN_EPS = 1e-5
ROPE_BASE = 10000.0
LANES = 128
VMEM_LIMIT = 56 * 1024 * 1024


def _params(sem, vmem=VMEM_LIMIT):
    return pltpu.CompilerParams(dimension_semantics=sem, vmem_limit_bytes=vmem)


def _layer_norm(y, g, b):
    mu = jnp.mean(y, axis=-1, keepdims=True)
    d = y - mu
    var = jnp.mean(d * d, axis=-1, keepdims=True)
    return d * lax.rsqrt(var + LN_EPS) * g + b


def _rms_norm(y):
    return y * lax.rsqrt(jnp.mean(y * y, axis=-1, keepdims=True) + LN_EPS)


def _silu(a):
    return a * (1.0 / (1.0 + jnp.exp(-a)))


def _dot_nt(a, b):
    return lax.dot_general(a, b, (((1,), (1,)), ((), ())), preferred_element_type=F32)


def _dot_tn(a, b):
    return lax.dot_general(a, b, (((0,), (0,)), ((), ())), preferred_element_type=F32)


def _dot(a, b):
    return jnp.dot(a, b, preferred_element_type=F32)


def _rope_kernel(inv_ref, cos_ref, sin_ref, *, base, period):
    rows = cos_ref.shape[0]
    row = lax.broadcasted_iota(jnp.int32, cos_ref.shape, 0)
    if period == 1:
        pos = jnp.full(cos_ref.shape, base, jnp.int32)
    else:
        assert period == rows
        pos = row + base
    ang = pos.astype(F32) * inv_ref[...]
    cos_ref[...] = jnp.cos(ang)
    sin_ref[...] = jnp.sin(ang)


def rope_table(rows, half, base, period):
    inv = (ROPE_BASE ** (-jnp.arange(half, dtype=F32) / half)).reshape(1, half)
    return pl.pallas_call(
        functools.partial(_rope_kernel, base=base, period=period),
        out_shape=(jax.ShapeDtypeStruct((rows, half), F32),) * 2,
        name="rope_table",
    )(inv)


def _layer_spec(w, layer, block, index_map):
    if w.ndim == 2:
        return pl.BlockSpec(block, index_map)
    return pl.BlockSpec((None,) + block, lambda *g: (layer,) + index_map(*g))


def _ffn_kernel(x_ref, w1_ref, w3_ref, w2_ref, gb_ref, o_ref, *rest, alpha, with_bf16):
    xb_ref = rest[-1]
    emit_out = rest[1:-1] if with_bf16 else rest[:-1]
    j = pl.program_id(1)

    @pl.when(j == 0)
    def _():
        xb_ref[...] = x_ref[...].astype(BF16)
        o_ref[...] = jnp.zeros_like(o_ref)

    w1, w3, w2 = (r[...].astype(BF16) for r in (w1_ref, w3_ref, w2_ref))
    for dst, w in zip(emit_out, (w1, w3, w2)):
        dst[...] = w

    xb = xb_ref[...]
    a = _dot(xb, w1)
    b = _dot(xb, w3)
    h = (_silu(a) * b).astype(BF16)
    o_ref[...] += _dot(h, w2)

    @pl.when(j == pl.num_programs(1) - 1)
    def _():
        y = alpha * x_ref[...] + 0.5 * o_ref[...]
        y = _layer_norm(y, gb_ref[0:1, :], gb_ref[1:2, :])
        o_ref[...] = y
        if with_bf16:
            rest[0][...] = y.astype(BF16)


def ffn_half(x, w1, w3, w2, g, b, *, layer, alpha, tm, tf, with_bf16):
    m, d = x.shape
    dff = w1.shape[-1]
    assert m % tm == 0 and dff % tf == 0
    up = ((d, tf), lambda i, j: (0, j))
    down = ((tf, d), lambda i, j: (j, 0))
    rows = pl.BlockSpec((tm, d), lambda i, j: (i, 0))
    out_specs = [rows]
    out_shape = [jax.ShapeDtypeStruct((m, d), F32)]
    if with_bf16:
        out_specs.append(rows)
        out_shape.append(jax.ShapeDtypeStruct((m, d), BF16))
    if w1.dtype != BF16:
        assert m == tm, "bf16 weight tiles are emitted once, by a single row tile"
        out_specs += [pl.BlockSpec(*up), pl.BlockSpec(*up), pl.BlockSpec(*down)]
        out_shape += [jax.ShapeDtypeStruct(w.shape[-2:], BF16) for w in (w1, w3, w2)]
    return pl.pallas_call(
        functools.partial(_ffn_kernel, alpha=alpha, with_bf16=with_bf16),
        grid=(m // tm, dff // tf),
        in_specs=[
            rows,
            _layer_spec(w1, layer, *up),
            _layer_spec(w3, layer, *up),
            _layer_spec(w2, layer, *down),
            pl.BlockSpec((2, d), lambda i, j: (0, 0)),
        ],
        out_specs=out_specs,
        out_shape=out_shape,
        scratch_shapes=[pltpu.VMEM((tm, d), BF16)],
        compiler_params=_params(("parallel", "arbitrary")),
        name="ffn_half",
    )(x, w1, w3, w2, jnp.stack([g, b]))


N_SEG = 7
SEG_KA, SEG_VA, SEG_QB, SEG_KB, SEG_VB, SEG_G = 1, 2, 3, 4, 5, 6


def _proj_kernel(xb_ref, w_ref, cos_ref, sin_ref, *rest, n_alias, q_scale, kb_scale, dh_b):
    h_ref, kf_ref, vf_ref, gf_ref = rest[n_alias:n_alias + 4]
    j = pl.program_id(1)

    w = w_ref[...].astype(BF16)
    for dst in rest[n_alias + 4:]:
        dst[...] = w
    acc = _dot(xb_ref[...], w)
    hd = h_ref.dtype

    @pl.when(j == 0)
    def _():
        h_ref[...] = (acc * q_scale).astype(hd)

    @pl.when(j == SEG_KA)
    def _():
        kf_ref[...] = acc
        h_ref[...] = acc.astype(hd)

    @pl.when(j == SEG_VA)
    def _():
        vf_ref[...] = acc
        h_ref[...] = acc.astype(hd)

    def rotary(scale):
        cos = cos_ref[...]
        sin = sin_ref[...]
        half = dh_b // 2
        for h in range(acc.shape[1] // dh_b):
            x1 = acc[:, h * dh_b:h * dh_b + half]
            x2 = acc[:, h * dh_b + half:(h + 1) * dh_b]
            h_ref[:, h * dh_b:h * dh_b + half] = ((x1 * cos - x2 * sin) * scale).astype(hd)
            h_ref[:, h * dh_b + half:(h + 1) * dh_b] = ((x2 * cos + x1 * sin) * scale).astype(hd)

    @pl.when(j == SEG_QB)
    def _():
        rotary(1.0)

    @pl.when(j == SEG_KB)
    def _():
        rotary(kb_scale)

    @pl.when(j == SEG_VB)
    def _():
        h_ref[...] = acc.astype(hd)

    @pl.when(j == SEG_G)
    def _():
        gf_ref[...] = acc


def in_proj(x, w_in, cos, sin, prev_kv, *, layer, depth, tm, h_dtype, q_scale, kb_scale, dh_b):
    m, d = x.shape
    seg = w_in.shape[-1] // N_SEG
    n_tab = cos.shape[0] // tm
    assert m % tm == 0 and cos.shape[0] % tm == 0
    kv_out = jax.ShapeDtypeStruct((depth, m, seg), F32)
    kv_spec = pl.BlockSpec((None, tm, seg), lambda i, j: (layer, i, 0))
    w_block = ((d, seg), lambda i, j: (0, j))
    in_specs = [
        pl.BlockSpec((tm, d), lambda i, j: (i, 0)),
        _layer_spec(w_in, layer, *w_block),
        pl.BlockSpec((tm, cos.shape[1]), lambda i, j: (i % n_tab, 0)),
        pl.BlockSpec((tm, cos.shape[1]), lambda i, j: (i % n_tab, 0)),
    ]
    args = [x, w_in, cos, sin]
    aliases = {}
    if prev_kv is not None:
        in_specs += [pl.BlockSpec(memory_space=pl.ANY)] * 2
        args += list(prev_kv)
        aliases = {4: 1, 5: 2}
    out_specs = [
        pl.BlockSpec((tm, seg), lambda i, j: (i, jnp.minimum(j, SEG_VB))),
        kv_spec,
        kv_spec,
        pl.BlockSpec((tm, seg), lambda i, j: (i, 0)),
    ]
    out_shape = [jax.ShapeDtypeStruct((m, SEG_G * seg), h_dtype), kv_out, kv_out,
                 jax.ShapeDtypeStruct((m, seg), F32)]
    if w_in.dtype != BF16:
        assert m == tm, "bf16 weight tiles are emitted once, by a single row tile"
        out_specs.append(pl.BlockSpec(*w_block))
        out_shape.append(jax.ShapeDtypeStruct(w_in.shape[-2:], BF16))
    return pl.pallas_call(
        functools.partial(_proj_kernel, n_alias=len(aliases), q_scale=q_scale, kb_scale=kb_scale,
                          dh_b=dh_b),
        grid=(m // tm, N_SEG),
        in_specs=in_specs,
        out_specs=out_specs,
        out_shape=out_shape,
        input_output_aliases=aliases,
        compiler_params=_params(("parallel", "arbitrary")),
        name="in_proj",
    )(*args)


PAST_SPAN = 2


def _diff_lambda(lam4_ref, lam_init):
    v = lam4_ref[...]
    t1 = jnp.sum(v[0:1] * v[1:2], axis=-1, keepdims=True)
    t2 = jnp.sum(v[2:3] * v[3:4], axis=-1, keepdims=True)
    return jnp.exp(t1) - jnp.exp(t2) + lam_init


def _prompt_attention(lam, g_ref, q_ref, k_ref, v_ref, o_ref,
                      vt_ref, m1_ref, l1_ref, a1_ref, m2_ref, l2_ref, a2_ref, *, tq, lam_init):
    qi = pl.program_id(2)
    tk = vt_ref.shape[2]
    per_tile = tq // tk

    @pl.when(qi == 0)
    def _():
        for j in range(vt_ref.shape[0]):
            vt_ref[j] = v_ref[j * tk:(j + 1) * tk, :].astype(F32).T.astype(BF16)

    q = q_ref[...]
    lane = lax.broadcasted_iota(jnp.int32, q.shape, 1)
    half = q.shape[1] // 2
    zero = jnp.zeros_like(q)
    q1 = jnp.where(lane < half, q, zero)
    q2 = jnp.where(lane >= half, q, zero)

    for m_ref, l_ref, a_ref in ((m1_ref, l1_ref, a1_ref), (m2_ref, l2_ref, a2_ref)):
        m_ref[...] = jnp.full(m_ref.shape, -jnp.inf, F32)
        l_ref[...] = jnp.zeros(l_ref.shape, F32)
        a_ref[...] = jnp.zeros(a_ref.shape, F32)

    def update(s, vt, m_ref, l_ref, a_ref):
        m_old = m_ref[...]
        m_new = jnp.maximum(m_old, jnp.max(s, axis=0, keepdims=True))
        alpha = jnp.exp2(m_old - m_new)
        p = jnp.exp2(s - m_new)
        l_ref[...] = alpha * l_ref[...] + jnp.sum(p, axis=0, keepdims=True)
        a_ref[...] = alpha * a_ref[...] + _dot(vt, p.astype(BF16))
        m_ref[...] = m_new

    def block(ki, span, diag):
        start = pl.multiple_of(ki * tk, tk)
        k = k_ref[pl.ds(start, span * tk), :]
        vt = jnp.concatenate([vt_ref[ki + u] for u in range(span)], axis=1)
        s1 = _dot_nt(k, q1)
        s2 = _dot_nt(k, q2)
        if diag is not None:
            r = lax.broadcasted_iota(jnp.int32, s1.shape, 0) + diag * tk
            c = lax.broadcasted_iota(jnp.int32, s1.shape, 1)
            s1 = jnp.where(r <= c, s1, -jnp.inf)
            s2 = jnp.where(r <= c, s2, -jnp.inf)
        update(s1, vt, m1_ref, l1_ref, a1_ref)
        update(s2, vt, m2_ref, l2_ref, a2_ref)

    past = qi * per_tile

    def body(kk, carry):
        block(kk * PAST_SPAN, PAST_SPAN, None)
        return carry

    lax.fori_loop(0, past // PAST_SPAN, body, 0)
    for u in range(PAST_SPAN - 1):
        @pl.when(past % PAST_SPAN > u)
        def _():
            block(past - past % PAST_SPAN + u, 1, None)
    for j in range(per_tile):
        block(past + j, 1, j)

    ot = a1_ref[...] / l1_ref[...] - lam * (a2_ref[...] / l2_ref[...])
    ot = ot * lax.rsqrt(jnp.mean(ot * ot, axis=0, keepdims=True) + LN_EPS)
    o_ref[...] = (ot.T * g_ref[...] * (1.0 - lam_init)).astype(o_ref.dtype)


def _sample_attention(lam, g_ref, q_ref, kn_ref, vn_ref, k_refs, v_refs, o_ref, *, lam_init):
    nh, dv = q_ref.shape[1], q_ref.shape[2]
    half = dv // 2

    q8 = q_ref[0].astype(F32)
    q16 = jnp.concatenate([q8, q8], axis=0)
    row = lax.broadcasted_iota(jnp.int32, q16.shape, 0)
    lane = lax.broadcasted_iota(jnp.int32, q16.shape, 1)
    qm = jnp.where(lane // half == row // nh, q16, 0.0).astype(BF16)

    kn = kn_ref[0].astype(BF16).astype(F32)
    vn = vn_ref[0].astype(BF16).astype(F32)
    s_self = jnp.sum(qm.astype(F32) * jnp.concatenate([kn, kn], axis=0), axis=-1, keepdims=True)

    n_col = k_refs[0].shape[0] * nh
    srow = lax.broadcasted_iota(jnp.int32, (2 * nh, n_col), 0)
    scol = lax.broadcasted_iota(jnp.int32, (2 * nh, n_col), 1)
    valid = (scol & (nh - 1)) == (srow & (nh - 1))

    scores = []
    for k_ref in k_refs:
        kf = k_ref[...].reshape(n_col, dv).astype(BF16)
        scores.append(jnp.where(valid, _dot_nt(qm, kf), -jnp.inf))
    mx = functools.reduce(jnp.maximum, scores)
    m = jnp.maximum(s_self, jnp.max(mx, axis=-1, keepdims=True))
    w_self = jnp.exp2(s_self - m)
    probs = [jnp.exp2(s - m) for s in scores]
    l = w_self + jnp.sum(functools.reduce(jnp.add, probs), axis=-1, keepdims=True)
    acc = w_self * jnp.concatenate([vn, vn], axis=0)
    for p, v_ref in zip(probs, v_refs):
        acc = acc + _dot(p.astype(BF16), v_ref[...].reshape(n_col, dv).astype(BF16))

    acc = acc / l
    o = acc[0:nh] - lam * acc[nh:2 * nh]
    o_ref[0] = _rms_norm(o) * g_ref[...] * (1.0 - lam_init)


def _attn_kernel(pt_ref, lam4_ref, g_ref, q_ref, k_ref, v_ref, qd_ref, knd_ref, vnd_ref, *rest,
                 n_pages, tq, lam_init):
    kd_refs = rest[:n_pages]
    vd_refs = rest[n_pages:2 * n_pages]
    o_ref, od_ref = rest[2 * n_pages:2 * n_pages + 2]
    scratch = rest[2 * n_pages + 2:]
    lam = _diff_lambda(lam4_ref, lam_init)
    _sample_attention(lam, g_ref, qd_ref, knd_ref, vnd_ref, kd_refs, vd_refs, od_ref,
                      lam_init=lam_init)
    _prompt_attention(lam, g_ref, q_ref, k_ref, v_ref, o_ref, *scratch, tq=tq, lam_init=lam_init)


def diff_attention(h, hd3, cache_k, cache_v, pt_flat, lam4, subln_g, *,
                   layer, bsz, s_len, n_pages, tq, tk, lam_init):
    _, _, page, nh, dv = cache_k.shape
    nq = s_len // tq
    assert s_len % tq == 0 and tq % tk == 0 and nh & (nh - 1) == 0
    assert hd3.shape[0] == bsz * nh * nq, "one sample sequence per prompt (batch, head, q tile)"

    def seq(b, hh, qi):
        return (b * nh + hh) * nq + qi

    def page_spec(r):
        return pl.BlockSpec(
            (None, None, page, nh, dv),
            lambda b, hh, qi, pt: (layer, pt[seq(b, hh, qi) * n_pages + r], 0, 0, 0))

    def new_token_spec(sg):
        return pl.BlockSpec((1, nh, dv), lambda b, hh, qi, pt: (seq(b, hh, qi), sg, 0))

    stat = pltpu.VMEM((1, tq), F32)
    accs = pltpu.VMEM((dv, tq), F32)
    grid_spec = pltpu.PrefetchScalarGridSpec(
        num_scalar_prefetch=1,
        grid=(bsz, nh, nq),
        in_specs=[
            pl.BlockSpec(lam4.shape, lambda b, hh, qi, pt: (0, 0)),
            pl.BlockSpec((1, dv), lambda b, hh, qi, pt: (0, 0)),
            pl.BlockSpec((tq, dv), lambda b, hh, qi, pt: (b * nq + qi, hh)),
            pl.BlockSpec((s_len, dv), lambda b, hh, qi, pt: (b, nh + hh)),
            pl.BlockSpec((s_len, dv), lambda b, hh, qi, pt: (b, 2 * nh + hh)),
            new_token_spec(0), new_token_spec(SEG_KA), new_token_spec(SEG_VA),
        ] + [page_spec(r) for r in range(n_pages)] * 2,
        out_specs=[
            pl.BlockSpec((tq, dv), lambda b, hh, qi, pt: (b * nq + qi, hh)),
            pl.BlockSpec((1, nh, dv), lambda b, hh, qi, pt: (seq(b, hh, qi), 0, 0)),
        ],
        scratch_shapes=[pltpu.VMEM((s_len // tk, dv, tk), BF16),
                        stat, stat, accs, stat, stat, accs],
    )
    return pl.pallas_call(
        functools.partial(_attn_kernel, n_pages=n_pages, tq=tq, lam_init=lam_init),
        grid_spec=grid_spec,
        out_shape=[jax.ShapeDtypeStruct((bsz * s_len, nh * dv), BF16),
                   jax.ShapeDtypeStruct((hd3.shape[0], nh, dv), F32)],
        compiler_params=_params(("parallel", "parallel", "arbitrary")),
        name="diff_attention",
    )(pt_flat, lam4, subln_g.reshape(1, dv), h, h, h, hd3, hd3, hd3,
      *([cache_k] * n_pages), *([cache_v] * n_pages))


def _log_gamma(h):
    return float(np.log(np.float32(1.0) - np.float32(2.0) ** np.float32(-5.0 - h)))


def _ret_prompt_kernel(q_ref, k_ref, v_ref, gate_ref, r_ref, sfin_ref,
                       st_ref, dm_ref, qd_ref, kd_ref, *, n_heads, dh):
    c = pl.program_id(0)
    bsz, ch = q_ref.shape[0], q_ref.shape[1]

    @pl.when(c == 0)
    def _():
        st_ref[...] = jnp.zeros_like(st_ref)
        i = lax.broadcasted_iota(jnp.int32, (ch, ch), 0)
        jj = lax.broadcasted_iota(jnp.int32, (ch, ch), 1)
        rel = (i - jj).astype(F32)
        idx = lax.broadcasted_iota(jnp.int32, (ch, 1), 0).astype(F32)
        for h in range(n_heads):
            lg = _log_gamma(h)
            dm_ref[h] = jnp.where(rel >= 0, jnp.exp(lg * jnp.maximum(rel, 0.0)), 0.0)
            qd_ref[h] = jnp.exp(lg * (idx + 1.0))
            kd_ref[h] = jnp.exp(lg * (ch - 1.0 - idx))

    for b in range(bsz):
        for h in range(n_heads):
            cols = slice(h * dh, (h + 1) * dh)
            q = q_ref[b, :, cols]
            k = k_ref[b, :, cols]
            v = v_ref[b, :, cols]
            s0 = st_ref[b * n_heads + h]
            inner = (_dot_nt(q, k) * dm_ref[h]).astype(BF16)
            o = _dot(inner, v) + qd_ref[h] * _dot(q, s0.astype(BF16))
            kk = (k.astype(F32) * kd_ref[h]).astype(BF16)
            chunk_decay = float(np.exp(np.float32(_log_gamma(h)) * np.float32(ch)))
            st_ref[b * n_heads + h] = s0 * chunk_decay + _dot_tn(kk, v)
            r_ref[b, :, cols] = (_rms_norm(o) * _silu(gate_ref[b, :, cols])).astype(r_ref.dtype)

    @pl.when(c == pl.num_programs(0) - 1)
    def _():
        for b in range(bsz):
            for h in range(n_heads):
                sfin_ref[b, h] = st_ref[b * n_heads + h]


def ret_prompt(h3, gate3, *, n_heads, dh, chunk):
    bsz, s_len, _ = h3.shape
    width = n_heads * dh
    assert s_len % chunk == 0

    def seg_spec(sg):
        return pl.BlockSpec((bsz, chunk, width), lambda c: (0, c, sg))

    return pl.pallas_call(
        functools.partial(_ret_prompt_kernel, n_heads=n_heads, dh=dh),
        grid=(s_len // chunk,),
        in_specs=[seg_spec(SEG_QB), seg_spec(SEG_KB), seg_spec(SEG_VB), seg_spec(0)],
        out_specs=[
            pl.BlockSpec((bsz, chunk, width), lambda c: (0, c, 0)),
            pl.BlockSpec((bsz, n_heads, dh, dh), lambda c: (0, 0, 0, 0)),
        ],
        out_shape=[jax.ShapeDtypeStruct((bsz, s_len, width), BF16),
                   jax.ShapeDtypeStruct((bsz, n_heads, dh, dh), F32)],
        scratch_shapes=[
            pltpu.VMEM((bsz * n_heads, dh, dh), F32),
            pltpu.VMEM((n_heads, chunk, chunk), F32),
            pltpu.VMEM((n_heads, chunk, 1), F32),
            pltpu.VMEM((n_heads, chunk, 1), F32),
        ],
        compiler_params=_params(("arbitrary",)),
        name="ret_prompt",
    )(h3, h3, h3, gate3)


def _ret_sample_kernel(q_ref, k_ref, v_ref, gate_ref, st_ref, *rest, n_heads, dh):
    r_ref, snew_ref = rest[-2:]
    rows = q_ref.shape[0]
    row = lax.broadcasted_iota(jnp.int32, (rows, dh), 0)

    gammas = [float(np.exp(np.float32(_log_gamma(h)))) for h in range(n_heads)]

    def body(bi, qs_acc):
        out = []
        for h in range(n_heads):
            cols = slice(h * dh, (h + 1) * dh)
            sel = row == bi
            qb = jnp.where(sel, q_ref[:, cols], 0.0).astype(BF16)
            kb = jnp.where(sel, k_ref[:, cols], 0.0).astype(BF16)
            vb = v_ref[:, cols].astype(BF16)
            s0 = st_ref[bi, h]
            snew_ref[bi, h] = s0 * gammas[h] + _dot_tn(kb, vb)
            out.append(qs_acc[h] + _dot(qb, s0.astype(BF16)))
        return tuple(out)

    qs = lax.fori_loop(0, rows, body, tuple(jnp.zeros((rows, dh), F32) for _ in range(n_heads)),
                       unroll=2)
    for h in range(n_heads):
        cols = slice(h * dh, (h + 1) * dh)
        q, k, v = (r[:, cols].astype(BF16).astype(F32) for r in (q_ref, k_ref, v_ref))
        o = jnp.sum(q * k, axis=-1, keepdims=True) * v + gammas[h] * qs[h]
        r_ref[:, cols] = _rms_norm(o) * _silu(gate_ref[:, cols])


def ret_sample(h2, gate, state, prev_out, *, layer, n_heads, dh, rows):
    bsz = h2.shape[0]
    width = n_heads * dh
    depth = state.shape[0]
    assert bsz % rows == 0
    aliased = prev_out is not None

    def seg_spec(sg):
        return pl.BlockSpec((rows, width), lambda i: (i, sg))

    st_spec = pl.BlockSpec((None, rows, n_heads, dh, dh), lambda i: (layer, i, 0, 0, 0))
    in_specs = [seg_spec(SEG_QB), seg_spec(SEG_KB), seg_spec(SEG_VB), seg_spec(0), st_spec]
    args = [h2, h2, h2, gate, state]
    if aliased:
        in_specs.append(pl.BlockSpec(memory_space=pl.ANY))
        args.append(prev_out)
    return pl.pallas_call(
        functools.partial(_ret_sample_kernel, n_heads=n_heads, dh=dh),
        grid=(bsz // rows,),
        in_specs=in_specs,
        out_specs=[pl.BlockSpec((rows, width), lambda i: (i, 0)), st_spec],
        out_shape=[jax.ShapeDtypeStruct((bsz, width), F32),
                   jax.ShapeDtypeStruct((depth, bsz, n_heads, dh, dh), F32)],
        input_output_aliases={5: 1} if aliased else {},
        compiler_params=_params(("parallel",)),
        name="ret_sample",
    )(*args)


def _out_proj_kernel(a_ref, r_ref, x_ref, w_ref, g_ref, b_ref, *rest, alpha, n_pass):
    o_ref = rest[n_pass]
    wa = a_ref.shape[1]
    tm = x_ref.shape[0]
    slab = min(tm, LANES)
    for r0 in range(0, tm, slab):
        rows = slice(r0, r0 + slab)
        mixed = (_dot(a_ref[rows, :].astype(BF16), w_ref[0:wa, :])
                 + _dot(r_ref[rows, :].astype(BF16), w_ref[wa:, :]))
        o_ref[rows, :] = _layer_norm(alpha * x_ref[rows, :] + mixed, g_ref[...], b_ref[...])


def out_proj(a, r, x, w_out, g, b, *, alpha, tm, passthrough=()):
    m, d = x.shape
    assert m % tm == 0
    n_pass = len(passthrough)
    any_spec = pl.BlockSpec(memory_space=pl.ANY)
    return pl.pallas_call(
        functools.partial(_out_proj_kernel, alpha=alpha, n_pass=n_pass),
        grid=(m // tm,),
        in_specs=[
            pl.BlockSpec((tm, a.shape[1]), lambda i: (i, 0)),
            pl.BlockSpec((tm, r.shape[1]), lambda i: (i, 0)),
            pl.BlockSpec((tm, d), lambda i: (i, 0)),
            pl.BlockSpec(w_out.shape, lambda i: (0, 0)),
            pl.BlockSpec((1, d), lambda i: (0, 0)),
            pl.BlockSpec((1, d), lambda i: (0, 0)),
        ] + [any_spec] * n_pass,
        out_specs=[pl.BlockSpec((tm, d), lambda i: (i, 0))] + [any_spec] * n_pass,
        out_shape=[jax.ShapeDtypeStruct((m, d), F32)]
        + [jax.ShapeDtypeStruct(p.shape, p.dtype) for p in passthrough],
        input_output_aliases={6 + k: 1 + k for k in range(n_pass)},
        compiler_params=_params(("parallel",)),
        name="out_proj",
    )(a, r, x, w_out, g.reshape(1, d), b.reshape(1, d), *passthrough)


ROW_TILE = 512
FF_TILE = 512
RET_CHUNK = 256
F32_SUBLANES = 8


def _row_tile(m, cap):
    t = min(m, cap)
    while m % t:
        t //= 2
    return t


def kernel(x_prompt, x_sample, cache_k, cache_v, state_ret, page_table, w_in, w_out,
           lambda_q1, lambda_k1, lambda_q2, lambda_k2, subln_g,
           ffn1_w1, ffn1_w3, ffn1_w2, ffn2_w1, ffn2_w3, ffn2_w2, ln_g, ln_b):
    bsz_p, s_p, d = x_prompt.shape
    bsz_d, s_d, _ = x_sample.shape
    assert s_d == 1, "one new token per sample sequence"
    depth, _, page, n_ha, dv = cache_k.shape
    n_hb, dh_b = state_ret.shape[2], state_ret.shape[3]
    n_pages = page_table.shape[1]
    past = n_pages * page
    dqk = dv // 2
    a_width = n_ha * dv
    b_width = n_hb * dh_b
    assert w_in.shape[2] == N_SEG * a_width and a_width == b_width
    dff = ffn1_w1.shape[2]
    alpha = (2 * depth) ** 0.25

    m_p = bsz_p * s_p
    m_d = bsz_d * s_d
    xp = x_prompt.reshape(m_p, d)
    xd = x_sample.reshape(m_d, d)

    cos_p, sin_p = rope_table(s_p, dh_b // 2, 0, s_p)
    cos_d, sin_d = rope_table(m_d, dh_b // 2, past, 1)

    w_out_b = w_out.astype(BF16)
    f1 = (ffn1_w1, ffn1_w3, ffn1_w2)
    f2 = (ffn2_w1, ffn2_w3, ffn2_w2)
    pt_flat = page_table.reshape(-1)

    tf = _row_tile(dff, FF_TILE)
    tm_ffn_p, tm_ffn_d = _row_tile(m_p, ROW_TILE), _row_tile(m_d, ROW_TILE)
    tm_out_p, tm_out_d = tm_ffn_p, tm_ffn_d
    tm_proj_p, tm_proj_d = _row_tile(s_p, 2 * ROW_TILE), _row_tile(m_d, ROW_TILE)
    tq = _row_tile(s_p, ROW_TILE)
    tk = tq
    chunk = _row_tile(s_p, RET_CHUNK)
    ret_rows = F32_SUBLANES

    proj_kw = dict(depth=depth, q_scale=dqk ** -0.5 * math.log2(math.e), kb_scale=dh_b ** -0.5,
                   dh_b=dh_b)
    ns_p = []
    kv_p = kv_d = ns_d = None
    for l in range(depth):
        lam_init = 0.8 - 0.6 * math.exp(-0.3 * l)
        lam4 = jnp.stack([lambda_q1[l], lambda_k1[l], lambda_q2[l], lambda_k2[l]]).astype(F32)
        ffn_kw = dict(layer=l, alpha=alpha, tf=tf)

        xd, xd_b, *ffn1_b = ffn_half(xd, *f1, ln_g[l, 0], ln_b[l, 0], tm=tm_ffn_d, with_bf16=True,
                                     **ffn_kw)
        hd, *kv_d, gf, w_in_b = in_proj(xd_b, w_in, cos_d, sin_d, kv_d, layer=l, tm=tm_proj_d,
                                        h_dtype=F32, **proj_kw)
        xp, xp_b = ffn_half(xp, *ffn1_b, ln_g[l, 0], ln_b[l, 0], tm=tm_ffn_p, with_bf16=True,
                            **ffn_kw)
        hp, *kv_p, gf_p = in_proj(xp_b, w_in_b, cos_p, sin_p, kv_p, layer=l, tm=tm_proj_p,
                                  h_dtype=BF16, **proj_kw)

        attn_p, attn_d = diff_attention(hp, hd.reshape(m_d, SEG_G * n_ha, dv), cache_k, cache_v, pt_flat,
                                        lam4, subln_g[l], layer=l, bsz=bsz_p, s_len=s_p,
                                        n_pages=n_pages, tq=tq, tk=tk, lam_init=lam_init)

        ret, ns_d = ret_sample(hd, gf, state_ret, ns_d, layer=l, n_heads=n_hb, dh=dh_b, rows=ret_rows)
        xd, = out_proj(attn_d.reshape(m_d, a_width), ret, xd, w_out_b[l], ln_g[l, 1], ln_b[l, 1],
                       alpha=alpha, tm=tm_out_d)
        xd, *ffn2_b = ffn_half(xd, *f2, ln_g[l, 2], ln_b[l, 2], tm=tm_ffn_d, with_bf16=False,
                               **ffn_kw)

        ret, s_fin = ret_prompt(hp.reshape(bsz_p, s_p, -1), gf_p.reshape(bsz_p, s_p, -1),
                                n_heads=n_hb, dh=dh_b, chunk=chunk)
        xp, *kv_done = out_proj(attn_p, ret.reshape(m_p, b_width), xp, w_out_b[l], ln_g[l, 1],
                                ln_b[l, 1], alpha=alpha, tm=tm_out_p,
                                passthrough=kv_p if l == depth - 1 else ())
        kv_p = kv_done or kv_p
        xp, = ffn_half(xp, *ffn2_b, ln_g[l, 2], ln_b[l, 2], tm=tm_ffn_p, with_bf16=False, **ffn_kw)
        ns_p.append(s_fin)

    kv_shape_p = (depth, bsz_p, s_p, n_ha, dv)
    kv_shape_d = (depth, bsz_d, s_d, n_ha, dv)
    return (xp.reshape(bsz_p, s_p, d), xd.reshape(bsz_d, s_d, d),
            kv_p[0].reshape(kv_shape_p), kv_p[1].reshape(kv_shape_p), jnp.stack(ns_p),
            kv_d[0].reshape(kv_shape_d), kv_d[1].reshape(kv_shape_d), ns_d)
```

```python
import functools
import math

import numpy as np
import jax
import jax.numpy as jnp
from jax import lax
from jax.experimental import pallas as pl
from jax.experimental.pallas import tpu as pltpu

F32 = jnp.float32
BF16 = jnp.bfloat16

LN_EPS = 1e-5
ROPE_BASE = 10000.0
LANES = 128
VMEM_LIMIT = 56 * 1024 * 1024


def _params(sem, vmem=VMEM_LIMIT):
    return pltpu.CompilerParams(dimension_semantics=sem, vmem_limit_bytes=vmem)


def _layer_norm(y, g, b):
    mu = jnp.mean(y, axis=-1, keepdims=True)
    d = y - mu
    var = jnp.mean(d * d, axis=-1, keepdims=True)
    return d * lax.rsqrt(var + LN_EPS) * g + b


def _rms_norm(y):
    return y * lax.rsqrt(jnp.mean(y * y, axis=-1, keepdims=True) + LN_EPS)


def _silu(a):
    return a * (1.0 / (1.0 + jnp.exp(-a)))


def _dot_nt(a, b):
    return lax.dot_general(a, b, (((1,), (1,)), ((), ())), preferred_element_type=F32)


def _dot_tn(a, b):
    return lax.dot_general(a, b, (((0,), (0,)), ((), ())), preferred_element_type=F32)


def _dot(a, b):
    return jnp.dot(a, b, preferred_element_type=F32)


def _rope_kernel(inv_ref, cos_ref, sin_ref, *, base, period):
    rows = cos_ref.shape[0]
    row = lax.broadcasted_iota(jnp.int32, cos_ref.shape, 0)
    if period == 1:
        pos = jnp.full(cos_ref.shape, base, jnp.int32)
    else:
        assert period == rows
        pos = row + base
    ang = pos.astype(F32) * inv_ref[...]
    cos_ref[...] = jnp.cos(ang)
    sin_ref[...] = jnp.sin(ang)


def rope_table(rows, half, base, period):
    inv = (ROPE_BASE ** (-jnp.arange(half, dtype=F32) / half)).reshape(1, half)
    return pl.pallas_call(
        functools.partial(_rope_kernel, base=base, period=period),
        out_shape=(jax.ShapeDtypeStruct((rows, half), F32),) * 2,
        name="rope_table",
    )(inv)


def _layer_spec(w, layer, block, index_map):
    if w.ndim == 2:
        return pl.BlockSpec(block, index_map)
    return pl.BlockSpec((None,) + block, lambda *g: (layer,) + index_map(*g))


def _ffn_kernel(x_ref, w1_ref, w3_ref, w2_ref, gb_ref, o_ref, *rest, alpha, with_bf16):
    xb_ref = rest[-1]
    emit_out = rest[1:-1] if with_bf16 else rest[:-1]
    j = pl.program_id(1)

    @pl.when(j == 0)
    def _():
        xb_ref[...] = x_ref[...].astype(BF16)
        o_ref[...] = jnp.zeros_like(o_ref)

    w1, w3, w2 = (r[...].astype(BF16) for r in (w1_ref, w3_ref, w2_ref))
    for dst, w in zip(emit_out, (w1, w3, w2)):
        dst[...] = w

    xb = xb_ref[...]
    a = _dot(xb, w1)
    b = _dot(xb, w3)
    h = (_silu(a) * b).astype(BF16)
    o_ref[...] += _dot(h, w2)

    @pl.when(j == pl.num_programs(1) - 1)
    def _():
        y = alpha * x_ref[...] + 0.5 * o_ref[...]
        y = _layer_norm(y, gb_ref[0:1, :], gb_ref[1:2, :])
        o_ref[...] = y
        if with_bf16:
            rest[0][...] = y.astype(BF16)


def ffn_half(x, w1, w3, w2, g, b, *, layer, alpha, tm, tf, with_bf16):
    m, d = x.shape
    dff = w1.shape[-1]
    assert m % tm == 0 and dff % tf == 0
    up = ((d, tf), lambda i, j: (0, j))
    down = ((tf, d), lambda i, j: (j, 0))
    rows = pl.BlockSpec((tm, d), lambda i, j: (i, 0))
    out_specs = [rows]
    out_shape = [jax.ShapeDtypeStruct((m, d), F32)]
    if with_bf16:
        out_specs.append(rows)
        out_shape.append(jax.ShapeDtypeStruct((m, d), BF16))
    if w1.dtype != BF16:
        assert m == tm, "bf16 weight tiles are emitted once, by a single row tile"
        out_specs += [pl.BlockSpec(*up), pl.BlockSpec(*up), pl.BlockSpec(*down)]
        out_shape += [jax.ShapeDtypeStruct(w.shape[-2:], BF16) for w in (w1, w3, w2)]
    return pl.pallas_call(
        functools.partial(_ffn_kernel, alpha=alpha, with_bf16=with_bf16),
        grid=(m // tm, dff // tf),
        in_specs=[
            rows,
            _layer_spec(w1, layer, *up),
            _layer_spec(w3, layer, *up),
            _layer_spec(w2, layer, *down),
            pl.BlockSpec((2, d), lambda i, j: (0, 0)),
        ],
        out_specs=out_specs,
        out_shape=out_shape,
        scratch_shapes=[pltpu.VMEM((tm, d), BF16)],
        compiler_params=_params(("parallel", "arbitrary")),
        name="ffn_half",
    )(x, w1, w3, w2, jnp.stack([g, b]))


N_SEG = 7
SEG_KA, SEG_VA, SEG_QB, SEG_KB, SEG_VB, SEG_G = 1, 2, 3, 4, 5, 6


def _proj_kernel(xb_ref, w_ref, cos_ref, sin_ref, *rest, n_alias, q_scale, kb_scale, dh_b):
    h_ref, kf_ref, vf_ref, gf_ref = rest[n_alias:n_alias + 4]
    j = pl.program_id(1)

    w = w_ref[...].astype(BF16)
    for dst in rest[n_alias + 4:]:
        dst[...] = w
    acc = _dot(xb_ref[...], w)
    hd = h_ref.dtype

    @pl.when(j == 0)
    def _():
        h_ref[...] = (acc * q_scale).astype(hd)

    @pl.when(j == SEG_KA)
    def _():
        kf_ref[...] = acc
        h_ref[...] = acc.astype(hd)

    @pl.when(j == SEG_VA)
    def _():
        vf_ref[...] = acc
        h_ref[...] = acc.astype(hd)

    def rotary(scale):
        cos = cos_ref[...]
        sin = sin_ref[...]
        half = dh_b // 2
        for h in range(acc.shape[1] // dh_b):
            x1 = acc[:, h * dh_b:h * dh_b + half]
            x2 = acc[:, h * dh_b + half:(h + 1) * dh_b]
            h_ref[:, h * dh_b:h * dh_b + half] = ((x1 * cos - x2 * sin) * scale).astype(hd)
            h_ref[:, h * dh_b + half:(h + 1) * dh_b] = ((x2 * cos + x1 * sin) * scale).astype(hd)

    @pl.when(j == SEG_QB)
    def _():
        rotary(1.0)

    @pl.when(j == SEG_KB)
    def _():
        rotary(kb_scale)

    @pl.when(j == SEG_VB)
    def _():
        h_ref[...] = acc.astype(hd)

    @pl.when(j == SEG_G)
    def _():
        gf_ref[...] = acc


def in_proj(x, w_in, cos, sin, prev_kv, *, layer, depth, tm, h_dtype, q_scale, kb_scale, dh_b):
    m, d = x.shape
    seg = w_in.shape[-1] // N_SEG
    n_tab = cos.shape[0] // tm
    assert m % tm == 0 and cos.shape[0] % tm == 0
    kv_out = jax.ShapeDtypeStruct((depth, m, seg), F32)
    kv_spec = pl.BlockSpec((None, tm, seg), lambda i, j: (layer, i, 0))
    w_block = ((d, seg), lambda i, j: (0, j))
    in_specs = [
        pl.BlockSpec((tm, d), lambda i, j: (i, 0)),
        _layer_spec(w_in, layer, *w_block),
        pl.BlockSpec((tm, cos.shape[1]), lambda i, j: (i % n_tab, 0)),
        pl.BlockSpec((tm, cos.shape[1]), lambda i, j: (i % n_tab, 0)),
    ]
    args = [x, w_in, cos, sin]
    aliases = {}
    if prev_kv is not None:
        in_specs += [pl.BlockSpec(memory_space=pl.ANY)] * 2
        args += list(prev_kv)
        aliases = {4: 1, 5: 2}
    out_specs = [
        pl.BlockSpec((tm, seg), lambda i, j: (i, jnp.minimum(j, SEG_VB))),
        kv_spec,
        kv_spec,
        pl.BlockSpec((tm, seg), lambda i, j: (i, 0)),
    ]
    out_shape = [jax.ShapeDtypeStruct((m, SEG_G * seg), h_dtype), kv_out, kv_out,
                 jax.ShapeDtypeStruct((m, seg), F32)]
    if w_in.dtype != BF16:
        assert m == tm, "bf16 weight tiles are emitted once, by a single row tile"
        out_specs.append(pl.BlockSpec(*w_block))
        out_shape.append(jax.ShapeDtypeStruct(w_in.shape[-2:], BF16))
    return pl.pallas_call(
        functools.partial(_proj_kernel, n_alias=len(aliases), q_scale=q_scale, kb_scale=kb_scale,
                          dh_b=dh_b),
        grid=(m // tm, N_SEG),
        in_specs=in_specs,
        out_specs=out_specs,
        out_shape=out_shape,
        input_output_aliases=aliases,
        compiler_params=_params(("parallel", "arbitrary")),
        name="in_proj",
    )(*args)


PAST_SPAN = 2


def _diff_lambda(lam4_ref, lam_init):
    v = lam4_ref[...]
    t1 = jnp.sum(v[0:1] * v[1:2], axis=-1, keepdims=True)
    t2 = jnp.sum(v[2:3] * v[3:4], axis=-1, keepdims=True)
    return jnp.exp(t1) - jnp.exp(t2) + lam_init


def _prompt_attention(lam, g_ref, q_ref, k_ref, v_ref, o_ref,
                      vt_ref, m1_ref, l1_ref, a1_ref, m2_ref, l2_ref, a2_ref, *, tq, lam_init):
    qi = pl.program_id(2)
    tk = vt_ref.shape[2]
    per_tile = tq // tk

    @pl.when(qi == 0)
    def _():
        for j in range(vt_ref.shape[0]):
            vt_ref[j] = v_ref[j * tk:(j + 1) * tk, :].astype(F32).T.astype(BF16)

    q = q_ref[...]
    lane = lax.broadcasted_iota(jnp.int32, q.shape, 1)
    half = q.shape[1] // 2
    zero = jnp.zeros_like(q)
    q1 = jnp.where(lane < half, q, zero)
    q2 = jnp.where(lane >= half, q, zero)

    for m_ref, l_ref, a_ref in ((m1_ref, l1_ref, a1_ref), (m2_ref, l2_ref, a2_ref)):
        m_ref[...] = jnp.full(m_ref.shape, -jnp.inf, F32)
        l_ref[...] = jnp.zeros(l_ref.shape, F32)
        a_ref[...] = jnp.zeros(a_ref.shape, F32)

    def update(s, vt, m_ref, l_ref, a_ref):
        m_old = m_ref[...]
        m_new = jnp.maximum(m_old, jnp.max(s, axis=0, keepdims=True))
        alpha = jnp.exp2(m_old - m_new)
        p = jnp.exp2(s - m_new)
        l_ref[...] = alpha * l_ref[...] + jnp.sum(p, axis=0, keepdims=True)
        a_ref[...] = alpha * a_ref[...] + _dot(vt, p.astype(BF16))
        m_ref[...] = m_new

    def block(ki, span, diag):
        start = pl.multiple_of(ki * tk, tk)
        k = k_ref[pl.ds(start, span * tk), :]
        vt = jnp.concatenate([vt_ref[ki + u] for u in range(span)], axis=1)
        s1 = _dot_nt(k, q1)
        s2 = _dot_nt(k, q2)
        if diag is not None:
            r = lax.broadcasted_iota(jnp.int32, s1.shape, 0) + diag * tk
            c = lax.broadcasted_iota(jnp.int32, s1.shape, 1)
            s1 = jnp.where(r <= c, s1, -jnp.inf)
            s2 = jnp.where(r <= c, s2, -jnp.inf)
        update(s1, vt, m1_ref, l1_ref, a1_ref)
        update(s2, vt, m2_ref, l2_ref, a2_ref)

    past = qi * per_tile

    def body(kk, carry):
        block(kk * PAST_SPAN, PAST_SPAN, None)
        return carry

    lax.fori_loop(0, past // PAST_SPAN, body, 0)
    for u in range(PAST_SPAN - 1):
        @pl.when(past % PAST_SPAN > u)
        def _():
            block(past - past % PAST_SPAN + u, 1, None)
    for j in range(per_tile):
        block(past + j, 1, j)

    ot = a1_ref[...] / l1_ref[...] - lam * (a2_ref[...] / l2_ref[...])
    ot = ot * lax.rsqrt(jnp.mean(ot * ot, axis=0, keepdims=True) + LN_EPS)
    o_ref[...] = (ot.T * g_ref[...] * (1.0 - lam_init)).astype(o_ref.dtype)


def _sample_attention(lam, g_ref, q_ref, kn_ref, vn_ref, k_refs, v_refs, o_ref, *, lam_init):
    nh, dv = q_ref.shape[1], q_ref.shape[2]
    half = dv // 2

    q8 = q_ref[0].astype(F32)
    q16 = jnp.concatenate([q8, q8], axis=0)
    row = lax.broadcasted_iota(jnp.int32, q16.shape, 0)
    lane = lax.broadcasted_iota(jnp.int32, q16.shape, 1)
    qm = jnp.where(lane // half == row // nh, q16, 0.0).astype(BF16)

    kn = kn_ref[0].astype(BF16).astype(F32)
    vn = vn_ref[0].astype(BF16).astype(F32)
    s_self = jnp.sum(qm.astype(F32) * jnp.concatenate([kn, kn], axis=0), axis=-1, keepdims=True)

    n_col = k_refs[0].shape[0] * nh
    srow = lax.broadcasted_iota(jnp.int32, (2 * nh, n_col), 0)
    scol = lax.broadcasted_iota(jnp.int32, (2 * nh, n_col), 1)
    valid = (scol & (nh - 1)) == (srow & (nh - 1))

    scores = []
    for k_ref in k_refs:
        kf = k_ref[...].reshape(n_col, dv).astype(BF16)
        scores.append(jnp.where(valid, _dot_nt(qm, kf), -jnp.inf))
    mx = functools.reduce(jnp.maximum, scores)
    m = jnp.maximum(s_self, jnp.max(mx, axis=-1, keepdims=True))
    w_self = jnp.exp2(s_self - m)
    probs = [jnp.exp2(s - m) for s in scores]
    l = w_self + jnp.sum(functools.reduce(jnp.add, probs), axis=-1, keepdims=True)
    acc = w_self * jnp.concatenate([vn, vn], axis=0)
    for p, v_ref in zip(probs, v_refs):
        acc = acc + _dot(p.astype(BF16), v_ref[...].reshape(n_col, dv).astype(BF16))

    acc = acc / l
    o = acc[0:nh] - lam * acc[nh:2 * nh]
    o_ref[0] = _rms_norm(o) * g_ref[...] * (1.0 - lam_init)


def _attn_kernel(pt_ref, lam4_ref, g_ref, q_ref, k_ref, v_ref, qd_ref, knd_ref, vnd_ref, *rest,
                 n_pages, tq, lam_init):
    kd_refs = rest[:n_pages]
    vd_refs = rest[n_pages:2 * n_pages]
    o_ref, od_ref = rest[2 * n_pages:2 * n_pages + 2]
    scratch = rest[2 * n_pages + 2:]
    lam = _diff_lambda(lam4_ref, lam_init)
    _sample_attention(lam, g_ref, qd_ref, knd_ref, vnd_ref, kd_refs, vd_refs, od_ref,
                      lam_init=lam_init)
    _prompt_attention(lam, g_ref, q_ref, k_ref, v_ref, o_ref, *scratch, tq=tq, lam_init=lam_init)


def diff_attention(h, hd3, cache_k, cache_v, pt_flat, lam4, subln_g, *,
                   layer, bsz, s_len, n_pages, tq, tk, lam_init):
    _, _, page, nh, dv = cache_k.shape
    nq = s_len // tq
    assert s_len % tq == 0 and tq % tk == 0 and nh & (nh - 1) == 0
    assert hd3.shape[0] == bsz * nh * nq, "one sample sequence per prompt (batch, head, q tile)"

    def seq(b, hh, qi):
        return (b * nh + hh) * nq + qi

    def page_spec(r):
        return pl.BlockSpec(
            (None, None, page, nh, dv),
            lambda b, hh, qi, pt: (layer, pt[seq(b, hh, qi) * n_pages + r], 0, 0, 0))

    def new_token_spec(sg):
        return pl.BlockSpec((1, nh, dv), lambda b, hh, qi, pt: (seq(b, hh, qi), sg, 0))

    stat = pltpu.VMEM((1, tq), F32)
    accs = pltpu.VMEM((dv, tq), F32)
    grid_spec = pltpu.PrefetchScalarGridSpec(
        num_scalar_prefetch=1,
        grid=(bsz, nh, nq),
        in_specs=[
            pl.BlockSpec(lam4.shape, lambda b, hh, qi, pt: (0, 0)),
            pl.BlockSpec((1, dv), lambda b, hh, qi, pt: (0, 0)),
            pl.BlockSpec((tq, dv), lambda b, hh, qi, pt: (b * nq + qi, hh)),
            pl.BlockSpec((s_len, dv), lambda b, hh, qi, pt: (b, nh + hh)),
            pl.BlockSpec((s_len, dv), lambda b, hh, qi, pt: (b, 2 * nh + hh)),
            new_token_spec(0), new_token_spec(SEG_KA), new_token_spec(SEG_VA),
        ] + [page_spec(r) for r in range(n_pages)] * 2,
        out_specs=[
            pl.BlockSpec((tq, dv), lambda b, hh, qi, pt: (b * nq + qi, hh)),
            pl.BlockSpec((1, nh, dv), lambda b, hh, qi, pt: (seq(b, hh, qi), 0, 0)),
        ],
        scratch_shapes=[pltpu.VMEM((s_len // tk, dv, tk), BF16),
                        stat, stat, accs, stat, stat, accs],
    )
    return pl.pallas_call(
        functools.partial(_attn_kernel, n_pages=n_pages, tq=tq, lam_init=lam_init),
        grid_spec=grid_spec,
        out_shape=[jax.ShapeDtypeStruct((bsz * s_len, nh * dv), BF16),
                   jax.ShapeDtypeStruct((hd3.shape[0], nh, dv), F32)],
        compiler_params=_params(("parallel", "parallel", "arbitrary")),
        name="diff_attention",
    )(pt_flat, lam4, subln_g.reshape(1, dv), h, h, h, hd3, hd3, hd3,
      *([cache_k] * n_pages), *([cache_v] * n_pages))


def _log_gamma(h):
    return float(np.log(np.float32(1.0) - np.float32(2.0) ** np.float32(-5.0 - h)))


def _ret_prompt_kernel(q_ref, k_ref, v_ref, gate_ref, r_ref, sfin_ref,
                       st_ref, dm_ref, qd_ref, kd_ref, *, n_heads, dh):
    c = pl.program_id(0)
    bsz, ch = q_ref.shape[0], q_ref.shape[1]

    @pl.when(c == 0)
    def _():
        st_ref[...] = jnp.zeros_like(st_ref)
        i = lax.broadcasted_iota(jnp.int32, (ch, ch), 0)
        jj = lax.broadcasted_iota(jnp.int32, (ch, ch), 1)
        rel = (i - jj).astype(F32)
        idx = lax.broadcasted_iota(jnp.int32, (ch, 1), 0).astype(F32)
        for h in range(n_heads):
            lg = _log_gamma(h)
            dm_ref[h] = jnp.where(rel >= 0, jnp.exp(lg * jnp.maximum(rel, 0.0)), 0.0)
            qd_ref[h] = jnp.exp(lg * (idx + 1.0))
            kd_ref[h] = jnp.exp(lg * (ch - 1.0 - idx))

    for b in range(bsz):
        for h in range(n_heads):
            cols = slice(h * dh, (h + 1) * dh)
            q = q_ref[b, :, cols]
            k = k_ref[b, :, cols]
            v = v_ref[b, :, cols]
            s0 = st_ref[b * n_heads + h]
            inner = (_dot_nt(q, k) * dm_ref[h]).astype(BF16)
            o = _dot(inner, v) + qd_ref[h] * _dot(q, s0.astype(BF16))
            kk = (k.astype(F32) * kd_ref[h]).astype(BF16)
            chunk_decay = float(np.exp(np.float32(_log_gamma(h)) * np.float32(ch)))
            st_ref[b * n_heads + h] = s0 * chunk_decay + _dot_tn(kk, v)
            r_ref[b, :, cols] = (_rms_norm(o) * _silu(gate_ref[b, :, cols])).astype(r_ref.dtype)

    @pl.when(c == pl.num_programs(0) - 1)
    def _():
        for b in range(bsz):
            for h in range(n_heads):
                sfin_ref[b, h] = st_ref[b * n_heads + h]


def ret_prompt(h3, gate3, *, n_heads, dh, chunk):
    bsz, s_len, _ = h3.shape
    width = n_heads * dh
    assert s_len % chunk == 0

    def seg_spec(sg):
        return pl.BlockSpec((bsz, chunk, width), lambda c: (0, c, sg))

    return pl.pallas_call(
        functools.partial(_ret_prompt_kernel, n_heads=n_heads, dh=dh),
        grid=(s_len // chunk,),
        in_specs=[seg_spec(SEG_QB), seg_spec(SEG_KB), seg_spec(SEG_VB), seg_spec(0)],
        out_specs=[
            pl.BlockSpec((bsz, chunk, width), lambda c: (0, c, 0)),
            pl.BlockSpec((bsz, n_heads, dh, dh), lambda c: (0, 0, 0, 0)),
        ],
        out_shape=[jax.ShapeDtypeStruct((bsz, s_len, width), BF16),
                   jax.ShapeDtypeStruct((bsz, n_heads, dh, dh), F32)],
        scratch_shapes=[
            pltpu.VMEM((bsz * n_heads, dh, dh), F32),
            pltpu.VMEM((n_heads, chunk, chunk), F32),
            pltpu.VMEM((n_heads, chunk, 1), F32),
            pltpu.VMEM((n_heads, chunk, 1), F32),
        ],
        compiler_params=_params(("arbitrary",)),
        name="ret_prompt",
    )(h3, h3, h3, gate3)


def _ret_sample_kernel(q_ref, k_ref, v_ref, gate_ref, st_ref, *rest, n_heads, dh):
    r_ref, snew_ref = rest[-2:]
    rows = q_ref.shape[0]
    row = lax.broadcasted_iota(jnp.int32, (rows, dh), 0)

    gammas = [float(np.exp(np.float32(_log_gamma(h)))) for h in range(n_heads)]

    def body(bi, qs_acc):
        out = []
        for h in range(n_heads):
            cols = slice(h * dh, (h + 1) * dh)
            sel = row == bi
            qb = jnp.where(sel, q_ref[:, cols], 0.0).astype(BF16)
            kb = jnp.where(sel, k_ref[:, cols], 0.0).astype(BF16)
            vb = v_ref[:, cols].astype(BF16)
            s0 = st_ref[bi, h]
            snew_ref[bi, h] = s0 * gammas[h] + _dot_tn(kb, vb)
            out.append(qs_acc[h] + _dot(qb, s0.astype(BF16)))
        return tuple(out)

    qs = lax.fori_loop(0, rows, body, tuple(jnp.zeros((rows, dh), F32) for _ in range(n_heads)),
                       unroll=2)
    for h in range(n_heads):
        cols = slice(h * dh, (h + 1) * dh)
        q, k, v = (r[:, cols].astype(BF16).astype(F32) for r in (q_ref, k_ref, v_ref))
        o = jnp.sum(q * k, axis=-1, keepdims=True) * v + gammas[h] * qs[h]
        r_ref[:, cols] = _rms_norm(o) * _silu(gate_ref[:, cols])


def ret_sample(h2, gate, state, prev_out, *, layer, n_heads, dh, rows):
    bsz = h2.shape[0]
    width = n_heads * dh
    depth = state.shape[0]
    assert bsz % rows == 0
    aliased = prev_out is not None

    def seg_spec(sg):
        return pl.BlockSpec((rows, width), lambda i: (i, sg))

    st_spec = pl.BlockSpec((None, rows, n_heads, dh, dh), lambda i: (layer, i, 0, 0, 0))
    in_specs = [seg_spec(SEG_QB), seg_spec(SEG_KB), seg_spec(SEG_VB), seg_spec(0), st_spec]
    args = [h2, h2, h2, gate, state]
    if aliased:
        in_specs.append(pl.BlockSpec(memory_space=pl.ANY))
        args.append(prev_out)
    return pl.pallas_call(
        functools.partial(_ret_sample_kernel, n_heads=n_heads, dh=dh),
        grid=(bsz // rows,),
        in_specs=in_specs,
        out_specs=[pl.BlockSpec((rows, width), lambda i: (i, 0)), st_spec],
        out_shape=[jax.ShapeDtypeStruct((bsz, width), F32),
                   jax.ShapeDtypeStruct((depth, bsz, n_heads, dh, dh), F32)],
        input_output_aliases={5: 1} if aliased else {},
        compiler_params=_params(("parallel",)),
        name="ret_sample",
    )(*args)


def _out_proj_kernel(a_ref, r_ref, x_ref, w_ref, g_ref, b_ref, *rest, alpha, n_unread):
    o_ref = rest[n_unread]
    wa = a_ref.shape[1]
    tm = x_ref.shape[0]
    slab = min(tm, LANES)
    for r0 in range(0, tm, slab):
        rows = slice(r0, r0 + slab)
        mixed = (_dot(a_ref[rows, :].astype(BF16), w_ref[0:wa, :])
                 + _dot(r_ref[rows, :].astype(BF16), w_ref[wa:, :]))
        o_ref[rows, :] = _layer_norm(alpha * x_ref[rows, :] + mixed, g_ref[...], b_ref[...])


def out_proj(a, r, x, w_out, g, b, *, alpha, tm, passthrough=(), after=()):
    m, d = x.shape
    assert m % tm == 0
    n_pass = len(passthrough)
    any_spec = pl.BlockSpec(memory_space=pl.ANY)
    return pl.pallas_call(
        functools.partial(_out_proj_kernel, alpha=alpha, n_unread=n_pass + len(after)),
        grid=(m // tm,),
        in_specs=[
            pl.BlockSpec((tm, a.shape[1]), lambda i: (i, 0)),
            pl.BlockSpec((tm, r.shape[1]), lambda i: (i, 0)),
            pl.BlockSpec((tm, d), lambda i: (i, 0)),
            pl.BlockSpec(w_out.shape, lambda i: (0, 0)),
            pl.BlockSpec((1, d), lambda i: (0, 0)),
            pl.BlockSpec((1, d), lambda i: (0, 0)),
        ] + [any_spec] * (n_pass + len(after)),
        out_specs=[pl.BlockSpec((tm, d), lambda i: (i, 0))] + [any_spec] * n_pass,
        out_shape=[jax.ShapeDtypeStruct((m, d), F32)]
        + [jax.ShapeDtypeStruct(p.shape, p.dtype) for p in passthrough],
        input_output_aliases={6 + k: 1 + k for k in range(n_pass)},
        compiler_params=_params(("parallel",)),
        name="out_proj",
    )(a, r, x, w_out, g.reshape(1, d), b.reshape(1, d), *passthrough, *after)


ROW_TILE = 512
FF_TILE = 512
RET_CHUNK = 256
F32_SUBLANES = 8


def _row_tile(m, cap):
    t = min(m, cap)
    while m % t:
        t //= 2
    return t


def kernel(x_prompt, x_sample, cache_k, cache_v, state_ret, page_table, w_in, w_out,
           lambda_q1, lambda_k1, lambda_q2, lambda_k2, subln_g,
           ffn1_w1, ffn1_w3, ffn1_w2, ffn2_w1, ffn2_w3, ffn2_w2, ln_g, ln_b):
    bsz_p, s_p, d = x_prompt.shape
    bsz_d, s_d, _ = x_sample.shape
    assert s_d == 1, "one new token per sample sequence"
    depth, _, page, n_ha, dv = cache_k.shape
    n_hb, dh_b = state_ret.shape[2], state_ret.shape[3]
    n_pages = page_table.shape[1]
    past = n_pages * page
    dqk = dv // 2
    a_width = n_ha * dv
    b_width = n_hb * dh_b
    assert w_in.shape[2] == N_SEG * a_width and a_width == b_width
    dff = ffn1_w1.shape[2]
    alpha = (2 * depth) ** 0.25

    m_p = bsz_p * s_p
    m_d = bsz_d * s_d
    xp = x_prompt.reshape(m_p, d)
    xd = x_sample.reshape(m_d, d)

    cos_p, sin_p = rope_table(s_p, dh_b // 2, 0, s_p)
    cos_d, sin_d = rope_table(m_d, dh_b // 2, past, 1)

    w_out_b = w_out.astype(BF16)
    f1 = (ffn1_w1, ffn1_w3, ffn1_w2)
    f2 = (ffn2_w1, ffn2_w3, ffn2_w2)
    pt_flat = page_table.reshape(-1)

    tf = _row_tile(dff, FF_TILE)
    tm_ffn_p, tm_ffn_d = _row_tile(m_p, ROW_TILE), _row_tile(m_d, ROW_TILE)
    tm_out_p, tm_out_d = tm_ffn_p, tm_ffn_d
    tm_proj_p, tm_proj_d = _row_tile(s_p, 2 * ROW_TILE), _row_tile(m_d, ROW_TILE)
    tq = _row_tile(s_p, ROW_TILE)
    tk = tq
    chunk = _row_tile(s_p, RET_CHUNK)
    ret_rows = F32_SUBLANES

    proj_kw = dict(depth=depth, q_scale=dqk ** -0.5 * math.log2(math.e), kb_scale=dh_b ** -0.5,
                   dh_b=dh_b)
    ns_p = []
    kv_p = kv_d = ns_d = None
    for l in range(depth):
        lam_init = 0.8 - 0.6 * math.exp(-0.3 * l)
        lam4 = jnp.stack([lambda_q1[l], lambda_k1[l], lambda_q2[l], lambda_k2[l]]).astype(F32)
        ffn_kw = dict(layer=l, alpha=alpha, tf=tf)

        xd, xd_b, *ffn1_b = ffn_half(xd, *f1, ln_g[l, 0], ln_b[l, 0], tm=tm_ffn_d, with_bf16=True,
                                     **ffn_kw)
        hd, *kv_d, gf, w_in_b = in_proj(xd_b, w_in, cos_d, sin_d, kv_d, layer=l, tm=tm_proj_d,
                                        h_dtype=F32, **proj_kw)
        xp, xp_b = ffn_half(xp, *ffn1_b, ln_g[l, 0], ln_b[l, 0], tm=tm_ffn_p, with_bf16=True,
                            **ffn_kw)
        hp, *kv_p, gf_p = in_proj(xp_b, w_in_b, cos_p, sin_p, kv_p, layer=l, tm=tm_proj_p,
                                  h_dtype=BF16, **proj_kw)

        attn_p, attn_d = diff_attention(hp, hd.reshape(m_d, SEG_G * n_ha, dv), cache_k, cache_v, pt_flat,
                                        lam4, subln_g[l], layer=l, bsz=bsz_p, s_len=s_p,
                                        n_pages=n_pages, tq=tq, tk=tk, lam_init=lam_init)

        ret, ns_d = ret_sample(hd, gf, state_ret, ns_d, layer=l, n_heads=n_hb, dh=dh_b, rows=ret_rows)
        xd, = out_proj(attn_d.reshape(m_d, a_width), ret, xd, w_out_b[l], ln_g[l, 1], ln_b[l, 1],
                       alpha=alpha, tm=tm_out_d)
        xd, *ffn2_b = ffn_half(xd, *f2, ln_g[l, 2], ln_b[l, 2], tm=tm_ffn_d, with_bf16=False,
                               **ffn_kw)

        ret, s_fin = ret_prompt(hp.reshape(bsz_p, s_p, -1), gf_p.reshape(bsz_p, s_p, -1),
                                n_heads=n_hb, dh=dh_b, chunk=chunk)
        last = l == depth - 1
        xp, *kv_done = out_proj(attn_p, ret.reshape(m_p, b_width), xp, w_out_b[l], ln_g[l, 1],
                                ln_b[l, 1], alpha=alpha, tm=tm_out_p,
                                passthrough=kv_p if last else (), after=(xd,) if last else ())
        kv_p = kv_done or kv_p
        xp, = ffn_half(xp, *ffn2_b, ln_g[l, 2], ln_b[l, 2], tm=tm_ffn_p, with_bf16=False, **ffn_kw)
        ns_p.append(s_fin)

    kv_shape_p = (depth, bsz_p, s_p, n_ha, dv)
    kv_shape_d = (depth, bsz_d, s_d, n_ha, dv)
    return (xp.reshape(bsz_p, s_p, d), xd.reshape(bsz_d, s_d, d),
            kv_p[0].reshape(kv_shape_p), kv_p[1].reshape(kv_shape_p), jnp.stack(ns_p),
            kv_d[0].reshape(kv_shape_d), kv_d[1].reshape(kv_shape_d), ns_d)
```
